```python
import math
import jax, jax.numpy as jnp
from jax import lax
import numpy as np

D_MODEL = 1024
BATCH = 16
SEQ = 2048
DEPTH = 4

N_MIXERS = 2
MLA_HEADS = 8
MLA_Q_RANK = 384
MLA_KV_RANK = 256
MLA_NOPE_DIM = 128
MLA_ROPE_DIM = 64
MLA_V_DIM = 128
ROPE_THETA = 10000.0
MOBA_HEADS = 8
MOBA_HEAD_DIM = D_MODEL // MOBA_HEADS
MOBA_BLOCK = 256
MOBA_TOP_K = 3
REL_BUCKETS = 32
REL_MAX_DISTANCE = 1024
D_FF = 4 * D_MODEL
Q_BLOCK = 128
RMS_EPS = 1e-6
N_MLA_LAYERS = (DEPTH + 1) // 2
N_MOBA_LAYERS = DEPTH // 2

kernel_name = "hybrid_mla_moba_sqrelu_trunk"


def rms_norm(x, g):
    xf = x.astype(jnp.float32)
    y = xf * lax.rsqrt(jnp.mean(xf * xf, axis=-1, keepdims=True) + RMS_EPS)
    return (y * g.astype(jnp.float32)).astype(x.dtype)


def rope_tables(positions):
    inv_freq = ROPE_THETA ** (-jnp.arange(0, MLA_ROPE_DIM, 2, dtype=jnp.float32) / MLA_ROPE_DIM)
    ang = positions.astype(jnp.float32)[..., None] * inv_freq
    return jnp.cos(ang), jnp.sin(ang)


def rope(x, cos, sin):
    half = x.shape[-1] // 2
    x1 = x[..., :half].astype(jnp.float32)
    x2 = x[..., half:].astype(jnp.float32)
    return jnp.concatenate([x1 * cos - x2 * sin, x2 * cos + x1 * sin], axis=-1).astype(x.dtype)


def t5_bucket(dist):
    n = jnp.maximum(dist, 0)
    max_exact = REL_BUCKETS // 2
    nf = jnp.maximum(n, 1).astype(jnp.float32)
    large = max_exact + (jnp.log(nf / max_exact) / math.log(REL_MAX_DISTANCE / max_exact)
                         * (REL_BUCKETS - max_exact)).astype(jnp.int32)
    large = jnp.minimum(large, REL_BUCKETS - 1)
    return jnp.where(n < max_exact, n, large)


def sq_relu_mlp(h, w_in, w_out):
    a = jax.nn.relu(h @ w_in)
    return (a * a) @ w_out


def mla_attention(q_nope, q_pe, k_nope, k_pe, v):
    B, H, S, _ = q_nope.shape
    nqb = S // Q_BLOCK
    scale = 1.0 / math.sqrt(MLA_NOPE_DIM + MLA_ROPE_DIM)
    k_idx = jnp.arange(S)

    def to_blocks(t):
        return jnp.moveaxis(t.reshape(B, H, nqb, Q_BLOCK, t.shape[-1]), 2, 0)

    def body(args):
        qn, qp, c = args
        logits = (jnp.einsum('bhqd,bhkd->bhqk', qn, k_nope)
                  + jnp.einsum('bhqd,bkd->bhqk', qp, k_pe)).astype(jnp.float32) * scale
        q_idx = c * Q_BLOCK + jnp.arange(Q_BLOCK)
        causal = k_idx[None, :] <= q_idx[:, None]
        logits = jnp.where(causal, logits, -jnp.inf)
        p = jax.nn.softmax(logits, axis=-1).astype(v.dtype)
        return jnp.einsum('bhqk,bhkd->bhqd', p, v)

    out = lax.map(body, (to_blocks(q_nope), to_blocks(q_pe), jnp.arange(nqb, dtype=jnp.int32)))
    return jnp.moveaxis(out, 0, 2).reshape(B, H, S, v.shape[-1])


def mla_mixer(h, w_in, q_a_norm, kv_a_norm, w_uq, w_ukv, q_nope_norm, q_rope_norm,
              k_nope_norm, k_rope_norm, w_o, cos, sin):
    B, S, _ = h.shape
    proj = h @ w_in
    c_q = proj[..., :MLA_Q_RANK]
    c_kv = proj[..., MLA_Q_RANK:MLA_Q_RANK + MLA_KV_RANK]
    k_pe = proj[..., MLA_Q_RANK + MLA_KV_RANK:]
    q = (rms_norm(c_q, q_a_norm) @ w_uq).reshape(B, S, MLA_HEADS, MLA_NOPE_DIM + MLA_ROPE_DIM)
    kv = (rms_norm(c_kv, kv_a_norm) @ w_ukv).reshape(B, S, MLA_HEADS, MLA_NOPE_DIM + MLA_V_DIM)
    q_nope = rms_norm(q[..., :MLA_NOPE_DIM], q_nope_norm)
    q_pe = rope(rms_norm(q[..., MLA_NOPE_DIM:], q_rope_norm), cos[:, :, None, :], sin[:, :, None, :])
    k_nope = rms_norm(kv[..., :MLA_NOPE_DIM], k_nope_norm)
    v = kv[..., MLA_NOPE_DIM:]
    k_pe = rope(rms_norm(k_pe, k_rope_norm), cos, sin)
    out = mla_attention(q_nope.transpose(0, 2, 1, 3), q_pe.transpose(0, 2, 1, 3),
                        k_nope.transpose(0, 2, 1, 3), k_pe, v.transpose(0, 2, 1, 3))
    return out.transpose(0, 2, 1, 3).reshape(B, S, MLA_HEADS * MLA_V_DIM) @ w_o


def moba_attention(q, k, v, positions, rel_bias_table):
    B, H, S, dh = q.shape
    nb = -(-S // MOBA_BLOCK)
    pad = nb * MOBA_BLOCK - S
    n_sel = min(MOBA_TOP_K, nb - 1)
    nqb = S // Q_BLOCK
    scale = 1.0 / math.sqrt(dh)
    kb = jnp.pad(k, ((0, 0), (0, 0), (0, pad), (0, 0))).reshape(B, H, nb, MOBA_BLOCK, dh)
    vb = jnp.pad(v, ((0, 0), (0, 0), (0, pad), (0, 0))).reshape(B, H, nb, MOBA_BLOCK, dh)
    k_mean = jnp.mean(kb.astype(jnp.float32), axis=3)
    pos_b = jnp.pad(positions, ((0, 0), (0, pad)), mode='edge').reshape(B, nb, MOBA_BLOCK)
    pos_q = jnp.moveaxis(positions.reshape(B, nqb, Q_BLOCK), 1, 0)
    q_blocks = jnp.moveaxis(q.reshape(B, H, nqb, Q_BLOCK, dh), 2, 0)
    table_h = rel_bias_table.T
    b_i = jnp.arange(B)[:, None, None]
    h_i = jnp.arange(H)[None, :, None]
    h_i4 = jnp.arange(H)[None, :, None, None]
    blk_ids = jnp.arange(nb)
    t_ids = jnp.arange(MOBA_BLOCK)

    def rel_bias(pq, pk):
        return table_h[h_i4, t5_bucket(pq - pk)].astype(jnp.float32)

    def body(args):
        qi, pq, c = args
        q_start = c * Q_BLOCK
        own = q_start // MOBA_BLOCK
        q_idx = q_start + jnp.arange(Q_BLOCK)
        pq4 = pq[:, None, :, None]
        parts = []
        sel = None
        if n_sel > 0:
            gate = jnp.einsum('bhqd,bhnd->bhqn', qi.astype(jnp.float32), k_mean)
            gate = jnp.where(blk_ids < own, gate, -jnp.inf)
            _, sel = lax.top_k(gate, n_sel)
            for r in range(n_sel):
                sel_r = sel[..., r]
                k_sel = kb[b_i, h_i, sel_r]
                lg = (jnp.einsum('bhqd,bhqtd->bhqt', qi, k_sel).astype(jnp.float32) * scale
                      + rel_bias(pq4, pos_b[b_i, sel_r]))
                parts.append(jnp.where(r < own, lg, -jnp.inf))
        k_own = lax.dynamic_index_in_dim(kb, own, axis=2, keepdims=False)
        v_own = lax.dynamic_index_in_dim(vb, own, axis=2, keepdims=False)
        pos_own = lax.dynamic_index_in_dim(pos_b, own, axis=1, keepdims=False)
        lg_own = (jnp.einsum('bhqd,bhtd->bhqt', qi, k_own).astype(jnp.float32) * scale
                  + rel_bias(pq4, pos_own[:, None, None, :]))
        causal = (own * MOBA_BLOCK + t_ids)[None, :] <= q_idx[:, None]
        parts.append(jnp.where(causal, lg_own, -jnp.inf))
        p = jax.nn.softmax(jnp.concatenate(parts, axis=-1), axis=-1).astype(v.dtype)
        p = p.reshape(B, H, Q_BLOCK, n_sel + 1, MOBA_BLOCK)
        out = jnp.einsum('bhqt,bhtd->bhqd', p[..., n_sel, :], v_own)
        for r in range(n_sel):
            v_sel = vb[b_i, h_i, sel[..., r]]
            out = out + jnp.einsum('bhqt,bhqtd->bhqd', p[..., r, :], v_sel)
        return out

    out = lax.map(body, (q_blocks, pos_q, jnp.arange(nqb, dtype=jnp.int32)))
    return jnp.moveaxis(out, 0, 2).reshape(B, H, S, dh)


def moba_mixer(h, w_qkv, q_norm, k_norm, w_o, positions, rel_bias_table):
    B, S, _ = h.shape
    qkv = (h @ w_qkv).reshape(B, S, 3, MOBA_HEADS, MOBA_HEAD_DIM)
    q = rms_norm(qkv[:, :, 0], q_norm).transpose(0, 2, 1, 3)
    k = rms_norm(qkv[:, :, 1], k_norm).transpose(0, 2, 1, 3)
    v = qkv[:, :, 2].transpose(0, 2, 1, 3)
    out = moba_attention(q, k, v, positions, rel_bias_table)
    return out.transpose(0, 2, 1, 3).reshape(B, S, MOBA_HEADS * MOBA_HEAD_DIM) @ w_o


def setup_inputs(seed: int = 0) -> dict:
    key = jax.random.key(seed)
    ks = jax.random.split(key, 24)

    def nrm(k, shape, scale):
        return scale * jax.random.normal(k, shape, jnp.float32)

    def gain(k, shape):
        return 1.0 + 0.05 * jax.random.normal(k, shape, jnp.float32)

    qk_dim = MLA_NOPE_DIM + MLA_ROPE_DIM
    offset = jax.random.randint(ks[1], (BATCH,), 0, 4096, dtype=jnp.int32)
    positions = offset[:, None] + jnp.arange(SEQ, dtype=jnp.int32)[None, :]
    return {
        "x": nrm(ks[0], (BATCH, SEQ, D_MODEL), 1.0),
        "positions": positions,
        "rel_bias_table": nrm(ks[2], (REL_BUCKETS, MOBA_HEADS), 0.5),
        "attn_norm": gain(ks[3], (DEPTH, D_MODEL)),
        "mlp_norm": gain(ks[4], (DEPTH, D_MODEL)),
        "mla_w_in": nrm(ks[5], (N_MLA_LAYERS, D_MODEL, MLA_Q_RANK + MLA_KV_RANK + MLA_ROPE_DIM), D_MODEL ** -0.5),
        "mla_q_a_norm": gain(ks[6], (N_MLA_LAYERS, MLA_Q_RANK)),
        "mla_kv_a_norm": gain(ks[7], (N_MLA_LAYERS, MLA_KV_RANK)),
        "mla_w_uq": nrm(ks[8], (N_MLA_LAYERS, MLA_Q_RANK, MLA_HEADS * qk_dim), MLA_Q_RANK ** -0.5),
        "mla_w_ukv": nrm(ks[9], (N_MLA_LAYERS, MLA_KV_RANK, MLA_HEADS * (MLA_NOPE_DIM + MLA_V_DIM)), MLA_KV_RANK ** -0.5),
        "mla_q_nope_norm": gain(ks[10], (N_MLA_LAYERS, MLA_NOPE_DIM)),
        "mla_q_rope_norm": gain(ks[11], (N_MLA_LAYERS, MLA_ROPE_DIM)),
        "mla_k_nope_norm": gain(ks[12], (N_MLA_LAYERS, MLA_NOPE_DIM)),
        "mla_k_rope_norm": gain(ks[13], (N_MLA_LAYERS, MLA_ROPE_DIM)),
        "mla_w_o": nrm(ks[14], (N_MLA_LAYERS, MLA_HEADS * MLA_V_DIM, D_MODEL), (MLA_HEADS * MLA_V_DIM) ** -0.5),
        "moba_w_qkv": nrm(ks[15], (N_MOBA_LAYERS, D_MODEL, 3 * MOBA_HEADS * MOBA_HEAD_DIM), D_MODEL ** -0.5),
        "moba_q_norm": gain(ks[16], (N_MOBA_LAYERS, MOBA_HEAD_DIM)),
        "moba_k_norm": gain(ks[17], (N_MOBA_LAYERS, MOBA_HEAD_DIM)),
        "moba_w_o": nrm(ks[18], (N_MOBA_LAYERS, MOBA_HEADS * MOBA_HEAD_DIM, D_MODEL), (MOBA_HEADS * MOBA_HEAD_DIM) ** -0.5),
        "mlp_w_in": nrm(ks[19], (DEPTH, D_MODEL, D_FF), D_MODEL ** -0.5),
        "mlp_w_out": nrm(ks[20], (DEPTH, D_FF, D_MODEL), D_FF ** -0.5),
    }


def reference(x, positions, rel_bias_table, attn_norm, mlp_norm, mla_w_in, mla_q_a_norm, mla_kv_a_norm,
              mla_w_uq, mla_w_ukv, mla_q_nope_norm, mla_q_rope_norm, mla_k_nope_norm, mla_k_rope_norm,
              mla_w_o, moba_w_qkv, moba_q_norm, moba_k_norm, moba_w_o, mlp_w_in, mlp_w_out):
    cos, sin = rope_tables(positions)
    for i in range(DEPTH):
        h = rms_norm(x, attn_norm[i])
        j = i // N_MIXERS
        if i % N_MIXERS == 0:
            x = x + mla_mixer(h, mla_w_in[j], mla_q_a_norm[j], mla_kv_a_norm[j], mla_w_uq[j], mla_w_ukv[j],
                              mla_q_nope_norm[j], mla_q_rope_norm[j], mla_k_nope_norm[j], mla_k_rope_norm[j],
                              mla_w_o[j], cos, sin)
        else:
            x = x + moba_mixer(h, moba_w_qkv[j], moba_q_norm[j], moba_k_norm[j], moba_w_o[j],
                               positions, rel_bias_table)
        h = rms_norm(x, mlp_norm[i])
        x = x + sq_relu_mlp(h, mlp_w_in[i], mlp_w_out[i])
    return x
```

```python
import functools
import math

import numpy as np
import jax
import jax.numpy as jnp
from jax import lax
from jax.experimental import pallas as pl
from jax.experimental.pallas import tpu as pltpu

MLA_HEADS = 8
MLA_NOPE_DIM = 128
MLA_ROPE_DIM = 64
MLA_V_DIM = 128
ROPE_THETA = 10000.0
MOBA_HEADS = 8
MOBA_BLOCK = 256
MOBA_TOP_K = 3
REL_MAX_DISTANCE = 1024
RMS_EPS = 1e-6

LANES = 128
VMEM_LIMIT_BYTES = 56 * 1024 * 1024

F32 = jnp.float32
BF16 = jnp.bfloat16
NEG_INF = float("-inf")


def _dot(a, b):
    return jnp.dot(a, b, preferred_element_type=F32)


def _dot_nt(a, b, precision=None):
    return lax.dot_general(a, b, (((1,), (1,)), ((), ())),
                           preferred_element_type=F32, precision=precision)


def _rms(x, g, n):
    ms = jnp.sum(x * x, axis=-1, keepdims=True) * (1.0 / n)
    return x * lax.rsqrt(ms + RMS_EPS) * g


def _params(*sem):
    return pltpu.CompilerParams(dimension_semantics=sem, vmem_limit_bytes=VMEM_LIMIT_BYTES)


def _rope_table_kernel(pos_ref, freq_ref, cos_ref, sin_ref):
    ang = pos_ref[...].astype(F32) * freq_ref[...]
    cos_ref[...] = jnp.cos(ang)
    sin_ref[...] = jnp.sin(ang)


def _rope_tables(positions):
    B, S = positions.shape
    half = MLA_ROPE_DIM // 2
    per_row = LANES // half
    inv_freq = ROPE_THETA ** (-jnp.arange(0, MLA_ROPE_DIM, 2, dtype=F32) / MLA_ROPE_DIM)
    rows = B * S // per_row
    pos_rep = jnp.repeat(positions.reshape(rows, per_row), half, axis=1)
    freq = jnp.tile(inv_freq, per_row).reshape(1, LANES)
    tr = min(rows, 1024)
    cos, sin = pl.pallas_call(
        _rope_table_kernel,
        grid=(rows // tr,),
        in_specs=[pl.BlockSpec((tr, LANES), lambda i: (i, 0)),
                  pl.BlockSpec((1, LANES), lambda i: (0, 0))],
        out_specs=[pl.BlockSpec((tr, LANES), lambda i: (i, 0))] * 2,
        out_shape=[jax.ShapeDtypeStruct((rows, LANES), F32)] * 2,
        compiler_params=_params("parallel"),
        name="rope_tables",
    )(pos_rep, freq)
    cos = cos.reshape(B, S, half)
    sin = sin.reshape(B, S, half)
    z = jnp.zeros_like(cos)
    cos_t = jnp.concatenate([cos, z, cos, z], axis=-1)
    sin_t = jnp.concatenate([-sin, z, sin, z], axis=-1)
    return cos_t, sin_t


def _mla_proj_kernel(x_ref, g_ref, w_in_ref, qa_g_ref, kva_g_ref, w_uq_ref, w_ukv_ref,
                     qn_g_ref, qr_g_ref, kn_g_ref, kr_g_ref, cos_ref, sin_ref,
                     q_out, k_out, v_out, *, q_rank, kv_rank):
    d_model = x_ref.shape[-1]
    n_heads = q_out.shape[1]
    h = _rms(x_ref[0], g_ref[...], d_model)
    proj = _dot(h.astype(BF16), w_in_ref[...])
    c_q = proj[:, :q_rank]
    c_kv = proj[:, q_rank:q_rank + kv_rank]
    k_pe = proj[:, q_rank + kv_rank:]
    qf = _dot(_rms(c_q, qa_g_ref[...], q_rank).astype(BF16), w_uq_ref[...])
    kvf = _dot(_rms(c_kv, kva_g_ref[...], kv_rank).astype(BF16), w_ukv_ref[...])
    cos = cos_ref[0]
    sin = sin_ref[0]

    def rope(t):
        return t * cos + pltpu.roll(t, LANES // 2, axis=1) * sin

    k_pe = rope(_rms(k_pe, kr_g_ref[...], MLA_ROPE_DIM)).astype(BF16)
    nope_w = n_heads * MLA_NOPE_DIM
    for hd in range(n_heads):
        lo = hd * LANES
        q_n = _rms(qf[:, lo:lo + LANES], qn_g_ref[...], MLA_NOPE_DIM)
        q_p = rope(_rms(qf[:, nope_w + lo:nope_w + lo + LANES], qr_g_ref[...], MLA_ROPE_DIM))
        q_out[0, hd] = jnp.concatenate([q_n.astype(BF16), q_p.astype(BF16)], axis=-1)
        k_n = _rms(kvf[:, lo:lo + LANES], kn_g_ref[...], MLA_NOPE_DIM)
        k_out[0, hd] = jnp.concatenate([k_n.astype(BF16), k_pe], axis=-1)
        v_out[0, hd] = kvf[:, nope_w + lo:nope_w + lo + LANES].astype(BF16)


def _mla_proj(x, g, w_in, qa_g, kva_g, w_uq, w_ukv, qn_g, qr_g, kn_g, kr_g, cos_t, sin_t, *, tm):
    B, S, D = x.shape
    H = MLA_HEADS
    q_rank = qa_g.shape[-1]
    kv_rank = kva_g.shape[-1]
    qk_w = 2 * LANES

    def full(a):
        return pl.BlockSpec(a.shape, lambda b, i: (0,) * a.ndim)

    tok = lambda w: pl.BlockSpec((1, tm, w), lambda b, i: (b, i, 0))
    head = lambda w: pl.BlockSpec((1, H, tm, w), lambda b, i: (b, 0, i, 0))
    return pl.pallas_call(
        functools.partial(_mla_proj_kernel, q_rank=q_rank, kv_rank=kv_rank),
        grid=(B, S // tm),
        in_specs=[tok(D), full(g), full(w_in), full(qa_g), full(kva_g), full(w_uq), full(w_ukv),
                  full(qn_g), full(qr_g), full(kn_g), full(kr_g), tok(LANES), tok(LANES)],
        out_specs=[head(qk_w), head(qk_w), head(MLA_V_DIM)],
        out_shape=[jax.ShapeDtypeStruct((B, H, S, qk_w), BF16),
                   jax.ShapeDtypeStruct((B, H, S, qk_w), BF16),
                   jax.ShapeDtypeStruct((B, H, S, MLA_V_DIM), BF16)],
        compiler_params=_params("parallel", "parallel"),
        name="mla_proj",
    )(x, g, w_in, qa_g, kva_g, w_uq, w_ukv, qn_g, qr_g, kn_g, kr_g, cos_t, sin_t)


def _mla_attn_kernel(q_ref, k_ref, v_ref, o_ref, *, tq, scale):
    S = q_ref.shape[2]
    row = lax.broadcasted_iota(jnp.int32, (tq, tq), 0)
    col = lax.broadcasted_iota(jnp.int32, (tq, tq), 1)
    causal = col <= row
    for i in range(S // tq):
        lo = i * tq
        q = q_ref[0, 0, lo:lo + tq, :]
        s_d = jnp.where(causal, _dot_nt(q, k_ref[0, 0, lo:lo + tq, :]) * scale, NEG_INF)
        m = jnp.max(s_d, axis=-1, keepdims=True)
        if i > 0:
            s_p = _dot_nt(q, k_ref[0, 0, :lo, :]) * scale
            m = jnp.maximum(m, jnp.max(s_p, axis=-1, keepdims=True))
        p_d = jnp.exp(s_d - m)
        l = jnp.sum(p_d, axis=-1, keepdims=True)
        acc = _dot(p_d.astype(BF16), v_ref[0, 0, lo:lo + tq, :])
        if i > 0:
            p_p = jnp.exp(s_p - m)
            l = l + jnp.sum(p_p, axis=-1, keepdims=True)
            acc = acc + _dot(p_p.astype(BF16), v_ref[0, 0, :lo, :])
        o_ref[0, lo:lo + tq, :] = (acc / l).astype(o_ref.dtype)


def _mla_attn(q, k, v, *, tq):
    B, H, S, qk_w = q.shape
    dv = v.shape[-1]
    scale = 1.0 / math.sqrt(MLA_NOPE_DIM + MLA_ROPE_DIM)
    spec = lambda w: pl.BlockSpec((1, 1, S, w), lambda b, h: (b, h, 0, 0))
    return pl.pallas_call(
        functools.partial(_mla_attn_kernel, tq=tq, scale=scale),
        grid=(B, H),
        in_specs=[spec(qk_w), spec(qk_w), spec(dv)],
        out_specs=pl.BlockSpec((1, S, dv), lambda b, h: (b, 0, h)),
        out_shape=jax.ShapeDtypeStruct((B, S, H * dv), BF16),
        compiler_params=_params("parallel", "parallel"),
        name="mla_attn",
    )(q, k, v)


def _moba_proj_kernel(x_ref, g_ref, w_ref, qg_ref, kg_ref, q_out, k_out, v_out, km_out, *, blk):
    d_model = x_ref.shape[-1]
    n_heads = q_out.shape[1]
    tm = x_ref.shape[1]
    dh = q_out.shape[-1]
    h = _rms(x_ref[0], g_ref[...], d_model)
    qkv = _dot(h.astype(BF16), w_ref[...])
    hw = n_heads * dh
    for hd in range(n_heads):
        lo = hd * dh
        q_out[0, hd] = _rms(qkv[:, lo:lo + dh], qg_ref[...], dh).astype(BF16)
        k_h = _rms(qkv[:, hw + lo:hw + lo + dh], kg_ref[...], dh)
        k_out[0, hd] = k_h.astype(BF16)
        v_out[0, hd] = qkv[:, 2 * hw + lo:2 * hw + lo + dh].astype(BF16)
        for j in range(tm // blk):
            km_out[0, j, hd:hd + 1, :] = jnp.mean(k_h[j * blk:(j + 1) * blk], axis=0, keepdims=True)


def _moba_proj(x, g, w_qkv, q_g, k_g, *, tm):
    B, S, D = x.shape
    H = MOBA_HEADS
    dh = q_g.shape[-1]
    nb = S // MOBA_BLOCK

    def full(a):
        return pl.BlockSpec(a.shape, lambda b, i: (0,) * a.ndim)

    head = pl.BlockSpec((1, H, tm, dh), lambda b, i: (b, 0, i, 0))
    return pl.pallas_call(
        functools.partial(_moba_proj_kernel, blk=MOBA_BLOCK),
        grid=(B, S // tm),
        in_specs=[pl.BlockSpec((1, tm, D), lambda b, i: (b, i, 0)),
                  full(g), full(w_qkv), full(q_g), full(k_g)],
        out_specs=[head, head, head,
                   pl.BlockSpec((1, tm // MOBA_BLOCK, H, dh), lambda b, i: (b, i, 0, 0))],
        out_shape=[jax.ShapeDtypeStruct((B, H, S, dh), BF16)] * 3
        + [jax.ShapeDtypeStruct((B, nb, H, dh), F32)],
        compiler_params=_params("parallel", "parallel"),
        name="moba_proj",
    )(x, g, w_qkv, q_g, k_g)


def _moba_attn_kernel(q_ref, k_ref, v_ref, km_ref, pq_ref, pk_ref, tbl_ref, o_ref,
                      m_scr, l_scr, acc_scr, mask_scr, *, top_k, scale, n_buckets):
    n_heads, blk, dh = q_ref.shape[1], q_ref.shape[2], q_ref.shape[3]
    nb = km_ref.shape[2]
    own = pl.program_id(1)
    pq = pq_ref[0]
    max_exact = n_buckets // 2
    log_ratio = math.log(REL_MAX_DISTANCE / max_exact)

    def bucket_of(pk):
        n = jnp.maximum(pq - pk, 0)
        nf = jnp.maximum(n, 1).astype(F32)
        large = max_exact + (jnp.log(nf / max_exact) / log_ratio
                             * (n_buckets - max_exact)).astype(jnp.int32)
        large = jnp.minimum(large, n_buckets - 1)
        return jnp.where(n < max_exact, n, large)

    def bias_of(hd, bucket):
        tbl = jnp.broadcast_to(tbl_ref[hd:hd + 1, :], (blk, LANES))
        parts = [jnp.take_along_axis(tbl, bucket[:, i * LANES:(i + 1) * LANES], axis=1,
                                     mode="promise_in_bounds") for i in range(blk // LANES)]
        return jnp.concatenate(parts, axis=1)

    def logits(hd, start, bucket):
        return _dot_nt(q_ref[0, hd], k_ref[0, hd, pl.ds(start, blk), :]) * scale + bias_of(hd, bucket)

    own_start = pl.multiple_of(own * blk, blk)
    bucket = bucket_of(pk_ref[0, own])
    row = lax.broadcasted_iota(jnp.int32, (blk, blk), 0)
    col = lax.broadcasted_iota(jnp.int32, (blk, blk), 1)
    causal = col <= row
    n_iota = lax.broadcasted_iota(jnp.int32, (blk, nb), 1)
    for hd in range(n_heads):
        s = jnp.where(causal, logits(hd, own_start, bucket), NEG_INF)
        m = jnp.max(s, axis=-1, keepdims=True)
        p = jnp.exp(s - m)
        m_scr[hd] = m
        l_scr[hd] = jnp.sum(p, axis=-1, keepdims=True)
        acc_scr[hd] = _dot(p.astype(BF16), v_ref[0, hd, pl.ds(own_start, blk), :])
        gate = _dot_nt(q_ref[0, hd].astype(F32), km_ref[0, hd], precision=lax.Precision.HIGHEST)
        cnt = jnp.zeros((blk, nb), F32)
        for mm in range(nb):
            g_m = gate[:, mm:mm + 1]
            above = (g_m > gate) | ((g_m == gate) & (mm < n_iota))
            cnt = cnt + jnp.where(above, jnp.where(mm < own, 1.0, 0.0), 0.0)
        keep = (cnt < top_k) & (n_iota < own)
        add_mask = jnp.where(keep, 0.0, NEG_INF)
        for n in range(nb):
            mask_scr[hd, n] = add_mask[:, n:n + 1]

    def past_block(n, carry):
        start = pl.multiple_of(n * blk, blk)
        bucket_n = bucket_of(pk_ref[0, n])
        for hd in range(n_heads):
            s = logits(hd, start, bucket_n) + mask_scr[hd, n]
            m_old = m_scr[hd]
            m_new = jnp.maximum(m_old, jnp.max(s, axis=-1, keepdims=True))
            alpha = jnp.exp(m_old - m_new)
            p = jnp.exp(s - m_new)
            l_scr[hd] = alpha * l_scr[hd] + jnp.sum(p, axis=-1, keepdims=True)
            acc_scr[hd] = alpha * acc_scr[hd] + _dot(p.astype(BF16), v_ref[0, hd, pl.ds(start, blk), :])
            m_scr[hd] = m_new
        return carry

    lax.fori_loop(0, own, past_block, 0)
    for hd in range(n_heads):
        o_ref[0, :, hd * dh:(hd + 1) * dh] = (acc_scr[hd] / l_scr[hd]).astype(o_ref.dtype)


def _moba_attn(q, k, v, k_mean, positions, table_h):
    B, H, S, dh = q.shape
    blk = MOBA_BLOCK
    nb = S // blk
    n_buckets = table_h.shape[1]
    tbl = jnp.pad(table_h, ((0, 0), (0, LANES - n_buckets)))
    pos_col = positions.reshape(B, S, 1)
    pos_blk = positions.reshape(B, nb, 1, blk)
    whole = lambda a: pl.BlockSpec((1,) + a.shape[1:], lambda b, c: (b,) + (0,) * (a.ndim - 1))
    return pl.pallas_call(
        functools.partial(_moba_attn_kernel, top_k=min(MOBA_TOP_K, nb - 1),
                          scale=1.0 / math.sqrt(dh), n_buckets=n_buckets),
        grid=(B, nb),
        in_specs=[pl.BlockSpec((1, H, blk, dh), lambda b, c: (b, 0, c, 0)),
                  whole(k), whole(v), whole(k_mean),
                  pl.BlockSpec((1, blk, 1), lambda b, c: (b, c, 0)),
                  whole(pos_blk),
                  pl.BlockSpec(tbl.shape, lambda b, c: (0, 0))],
        out_specs=pl.BlockSpec((1, blk, H * dh), lambda b, c: (b, c, 0)),
        out_shape=jax.ShapeDtypeStruct((B, S, H * dh), BF16),
        scratch_shapes=[pltpu.VMEM((H, blk, 1), F32), pltpu.VMEM((H, blk, 1), F32),
                        pltpu.VMEM((H, blk, dh), F32), pltpu.VMEM((H, nb, blk, 1), F32)],
        compiler_params=_params("parallel", "arbitrary"),
        name="moba_attn",
    )(q, k, v, k_mean, pos_col, pos_blk, tbl)


def _oproj_mlp_kernel(x_ref, a_ref, wo_ref, g_ref, win_ref, wout_ref, o_ref, h_scr):
    @pl.when(pl.program_id(1) == 0)
    def _():
        x1 = x_ref[...] + _dot(a_ref[...], wo_ref[...])
        o_ref[...] = x1
        h_scr[...] = _rms(x1, g_ref[...], x1.shape[-1]).astype(BF16)

    a = jnp.maximum(_dot(h_scr[...], win_ref[...]), 0.0)
    o_ref[...] += _dot((a * a).astype(BF16), wout_ref[...])


def _oproj_mlp(x, attn, w_o, g, w_in, w_out, *, tm, tf):
    T, D = x.shape
    F = w_in.shape[1]
    return pl.pallas_call(
        _oproj_mlp_kernel,
        grid=(T // tm, F // tf),
        in_specs=[pl.BlockSpec((tm, D), lambda i, j: (i, 0)),
                  pl.BlockSpec((tm, attn.shape[1]), lambda i, j: (i, 0)),
                  pl.BlockSpec(w_o.shape, lambda i, j: (0, 0)),
                  pl.BlockSpec(g.shape, lambda i, j: (0, 0)),
                  pl.BlockSpec((D, tf), lambda i, j: (0, j)),
                  pl.BlockSpec((tf, D), lambda i, j: (j, 0))],
        out_specs=pl.BlockSpec((tm, D), lambda i, j: (i, 0)),
        out_shape=jax.ShapeDtypeStruct((T, D), F32),
        scratch_shapes=[pltpu.VMEM((tm, D), BF16)],
        compiler_params=_params("parallel", "arbitrary"),
        name="oproj_mlp",
    )(x, attn, w_o, g, w_in, w_out)


def _gather_cols(w, src):
    src = np.asarray(src)
    cols = jnp.take(w, jnp.asarray(np.maximum(src, 0)), axis=-1)
    return jnp.where(jnp.asarray(src >= 0), cols, jnp.zeros((), w.dtype))


def _rope_group_src(base):
    half = MLA_ROPE_DIM // 2
    src = np.full((LANES,), -1, np.int64)
    src[:half] = base + np.arange(half)
    src[LANES // 2:LANES // 2 + half] = base + half + np.arange(half)
    return src


def _mla_layouts(q_rank, kv_rank):
    H = MLA_HEADS
    qk = MLA_NOPE_DIM + MLA_ROPE_DIM
    kvw = MLA_NOPE_DIM + MLA_V_DIM
    in_src = np.concatenate([np.arange(q_rank + kv_rank), _rope_group_src(q_rank + kv_rank)])
    uq_src = np.concatenate(
        [h * qk + np.arange(MLA_NOPE_DIM) for h in range(H)]
        + [_rope_group_src(h * qk + MLA_NOPE_DIM) for h in range(H)])
    ukv_src = np.concatenate(
        [h * kvw + np.arange(MLA_NOPE_DIM) for h in range(H)]
        + [h * kvw + MLA_NOPE_DIM + np.arange(MLA_V_DIM) for h in range(H)])
    return in_src, uq_src, ukv_src, _rope_group_src(0)


def kernel(x, positions, rel_bias_table, attn_norm, mlp_norm, mla_w_in, mla_q_a_norm, mla_kv_a_norm,
           mla_w_uq, mla_w_ukv, mla_q_nope_norm, mla_q_rope_norm, mla_k_nope_norm, mla_k_rope_norm,
           mla_w_o, moba_w_qkv, moba_q_norm, moba_k_norm, moba_w_o, mlp_w_in, mlp_w_out):
    B, S, D = x.shape
    depth = attn_norm.shape[0]
    q_rank = mla_q_a_norm.shape[-1]
    kv_rank = mla_kv_a_norm.shape[-1]
    assert S % MOBA_BLOCK == 0 and mla_w_in.shape[-1] == q_rank + kv_rank + MLA_ROPE_DIM
    in_src, uq_src, ukv_src, rope_src = _mla_layouts(q_rank, kv_rank)
    row = lambda v: v.reshape(1, -1)
    cos_t, sin_t = _rope_tables(positions)
    table_h = rel_bias_table.T
    for i in range(depth):
        j = i // 2
        if i % 2 == 0:
            q, k, v = _mla_proj(
                x, row(attn_norm[i]), _gather_cols(mla_w_in[j], in_src).astype(BF16),
                row(mla_q_a_norm[j]), row(mla_kv_a_norm[j]),
                _gather_cols(mla_w_uq[j], uq_src).astype(BF16),
                _gather_cols(mla_w_ukv[j], ukv_src).astype(BF16),
                row(mla_q_nope_norm[j]), row(_gather_cols(mla_q_rope_norm[j], rope_src)),
                row(mla_k_nope_norm[j]), row(_gather_cols(mla_k_rope_norm[j], rope_src)),
                cos_t, sin_t, tm=512)
            attn = _mla_attn(q, k, v, tq=256)
            w_o = mla_w_o[j]
        else:
            q, k, v, k_mean = _moba_proj(x, row(attn_norm[i]), moba_w_qkv[j].astype(BF16),
                                         row(moba_q_norm[j]), row(moba_k_norm[j]), tm=512)
            attn = _moba_attn(q, k, v, k_mean.transpose(0, 2, 1, 3), positions, table_h)
            w_o = moba_w_o[j]
        x = _oproj_mlp(x.reshape(B * S, D), attn.reshape(B * S, -1), w_o.astype(BF16),
                       row(mlp_norm[i]), mlp_w_in[i].astype(BF16), mlp_w_out[i].astype(BF16),
                       tm=1024, tf=512).reshape(B, S, D)
    return x
```

```python
import functools
import math

import numpy as np
import jax
import jax.numpy as jnp
from jax import lax
from jax.experimental import pallas as pl
from jax.experimental.pallas import tpu as pltpu

MLA_HEADS = 8
MLA_NOPE_DIM = 128
MLA_ROPE_DIM = 64
MLA_V_DIM = 128
ROPE_THETA = 10000.0
MOBA_HEADS = 8
MOBA_BLOCK = 256
MOBA_TOP_K = 3
REL_MAX_DISTANCE = 1024
RMS_EPS = 1e-6

LANES = 128
VMEM_LIMIT_BYTES = 56 * 1024 * 1024

F32 = jnp.float32
BF16 = jnp.bfloat16
NEG_INF = float("-inf")


def _dot(a, b):
    return jnp.dot(a, b, preferred_element_type=F32)


def _dot_nt(a, b, precision=None):
    return lax.dot_general(a, b, (((1,), (1,)), ((), ())),
                           preferred_element_type=F32, precision=precision)


def _rms(x, g, n):
    ms = jnp.sum(x * x, axis=-1, keepdims=True) * (1.0 / n)
    return x * lax.rsqrt(ms + RMS_EPS) * g


def _params(*sem):
    return pltpu.CompilerParams(dimension_semantics=sem, vmem_limit_bytes=VMEM_LIMIT_BYTES)


def _rope_table_kernel(pos_ref, freq_ref, cos_ref, sin_ref):
    ang = pos_ref[...].astype(F32) * freq_ref[...]
    cos_ref[...] = jnp.cos(ang)
    sin_ref[...] = jnp.sin(ang)


def _rope_tables(positions):
    B, S = positions.shape
    half = MLA_ROPE_DIM // 2
    per_row = LANES // half
    inv_freq = ROPE_THETA ** (-jnp.arange(0, MLA_ROPE_DIM, 2, dtype=F32) / MLA_ROPE_DIM)
    rows = B * S // per_row
    pos_rep = jnp.repeat(positions.reshape(rows, per_row), half, axis=1)
    freq = jnp.tile(inv_freq, per_row).reshape(1, LANES)
    tr = min(rows, 1024)
    cos, sin = pl.pallas_call(
        _rope_table_kernel,
        grid=(rows // tr,),
        in_specs=[pl.BlockSpec((tr, LANES), lambda i: (i, 0)),
                  pl.BlockSpec((1, LANES), lambda i: (0, 0))],
        out_specs=[pl.BlockSpec((tr, LANES), lambda i: (i, 0))] * 2,
        out_shape=[jax.ShapeDtypeStruct((rows, LANES), F32)] * 2,
        compiler_params=_params("parallel"),
        name="rope_tables",
    )(pos_rep, freq)
    cos = cos.reshape(B, S, half)
    sin = sin.reshape(B, S, half)
    z = jnp.zeros_like(cos)
    cos_t = jnp.concatenate([cos, z, cos, z], axis=-1)
    sin_t = jnp.concatenate([-sin, z, sin, z], axis=-1)
    return cos_t, sin_t


def _mla_proj_kernel(x_ref, g_ref, w_in_ref, qa_g_ref, kva_g_ref, w_uq_ref, w_ukv_ref,
                     qn_g_ref, qr_g_ref, kn_g_ref, kr_g_ref, cos_ref, sin_ref,
                     q_out, k_out, v_out, *, q_rank, kv_rank):
    d_model = x_ref.shape[-1]
    n_heads = q_out.shape[1]
    h = _rms(x_ref[0], g_ref[...], d_model)
    proj = _dot(h.astype(BF16), w_in_ref[...])
    c_q = proj[:, :q_rank]
    c_kv = proj[:, q_rank:q_rank + kv_rank]
    k_pe = proj[:, q_rank + kv_rank:]
    qf = _dot(_rms(c_q, qa_g_ref[...], q_rank).astype(BF16), w_uq_ref[...])
    kvf = _dot(_rms(c_kv, kva_g_ref[...], kv_rank).astype(BF16), w_ukv_ref[...])
    cos = cos_ref[0]
    sin = sin_ref[0]

    def rope(t):
        return t * cos + pltpu.roll(t, LANES // 2, axis=1) * sin

    k_pe = rope(_rms(k_pe, kr_g_ref[...], MLA_ROPE_DIM)).astype(BF16)
    nope_w = n_heads * MLA_NOPE_DIM
    for hd in range(n_heads):
        lo = hd * LANES
        q_n = _rms(qf[:, lo:lo + LANES], qn_g_ref[...], MLA_NOPE_DIM)
        q_p = rope(_rms(qf[:, nope_w + lo:nope_w + lo + LANES], qr_g_ref[...], MLA_ROPE_DIM))
        q_out[0, hd] = jnp.concatenate([q_n.astype(BF16), q_p.astype(BF16)], axis=-1)
        k_n = _rms(kvf[:, lo:lo + LANES], kn_g_ref[...], MLA_NOPE_DIM)
        k_out[0, hd] = jnp.concatenate([k_n.astype(BF16), k_pe], axis=-1)
        v_out[0, hd] = kvf[:, nope_w + lo:nope_w + lo + LANES].astype(BF16)


def _mla_proj(x, g, w_in, qa_g, kva_g, w_uq, w_ukv, qn_g, qr_g, kn_g, kr_g, cos_t, sin_t, *, tm):
    B, S, D = x.shape
    H = MLA_HEADS
    q_rank = qa_g.shape[-1]
    kv_rank = kva_g.shape[-1]
    qk_w = 2 * LANES

    def full(a):
        return pl.BlockSpec(a.shape, lambda b, i: (0,) * a.ndim)

    tok = lambda w: pl.BlockSpec((1, tm, w), lambda b, i: (b, i, 0))
    head = lambda w: pl.BlockSpec((1, H, tm, w), lambda b, i: (b, 0, i, 0))
    return pl.pallas_call(
        functools.partial(_mla_proj_kernel, q_rank=q_rank, kv_rank=kv_rank),
        grid=(B, S // tm),
        in_specs=[tok(D), full(g), full(w_in), full(qa_g), full(kva_g), full(w_uq), full(w_ukv),
                  full(qn_g), full(qr_g), full(kn_g), full(kr_g), tok(LANES), tok(LANES)],
        out_specs=[head(qk_w), head(qk_w), head(MLA_V_DIM)],
        out_shape=[jax.ShapeDtypeStruct((B, H, S, qk_w), BF16),
                   jax.ShapeDtypeStruct((B, H, S, qk_w), BF16),
                   jax.ShapeDtypeStruct((B, H, S, MLA_V_DIM), BF16)],
        compiler_params=_params("parallel", "parallel"),
        name="mla_proj",
    )(x, g, w_in, qa_g, kva_g, w_uq, w_ukv, qn_g, qr_g, kn_g, kr_g, cos_t, sin_t)


def _mla_attn_kernel(q_ref, k_ref, v_ref, o_ref, *, tq, scale):
    S = q_ref.shape[2]
    row = lax.broadcasted_iota(jnp.int32, (tq, tq), 0)
    col = lax.broadcasted_iota(jnp.int32, (tq, tq), 1)
    causal = col <= row
    for i in range(S // tq):
        lo = i * tq
        q = q_ref[0, 0, lo:lo + tq, :]
        s_d = jnp.where(causal, _dot_nt(q, k_ref[0, 0, lo:lo + tq, :]) * scale, NEG_INF)
        m = jnp.max(s_d, axis=-1, keepdims=True)
        if i > 0:
            s_p = _dot_nt(q, k_ref[0, 0, :lo, :]) * scale
            m = jnp.maximum(m, jnp.max(s_p, axis=-1, keepdims=True))
        p_d = jnp.exp(s_d - m)
        l = jnp.sum(p_d, axis=-1, keepdims=True)
        acc = _dot(p_d.astype(BF16), v_ref[0, 0, lo:lo + tq, :])
        if i > 0:
            p_p = jnp.exp(s_p - m)
            l = l + jnp.sum(p_p, axis=-1, keepdims=True)
            acc = acc + _dot(p_p.astype(BF16), v_ref[0, 0, :lo, :])
        o_ref[0, lo:lo + tq, :] = (acc / l).astype(o_ref.dtype)


def _mla_attn(q, k, v, *, tq):
    B, H, S, qk_w = q.shape
    dv = v.shape[-1]
    scale = 1.0 / math.sqrt(MLA_NOPE_DIM + MLA_ROPE_DIM)
    spec = lambda w: pl.BlockSpec((1, 1, S, w), lambda b, h: (b, h, 0, 0))
    return pl.pallas_call(
        functools.partial(_mla_attn_kernel, tq=tq, scale=scale),
        grid=(B, H),
        in_specs=[spec(qk_w), spec(qk_w), spec(dv)],
        out_specs=pl.BlockSpec((1, S, dv), lambda b, h: (b, 0, h)),
        out_shape=jax.ShapeDtypeStruct((B, S, H * dv), BF16),
        compiler_params=_params("parallel", "parallel"),
        name="mla_attn",
    )(q, k, v)


def _moba_proj_kernel(x_ref, g_ref, wqk_ref, wvt_ref, qg_ref, kg_ref, q_out, k_out, vt_out, km_out, *, blk):
    d_model = x_ref.shape[-1]
    n_heads = q_out.shape[1]
    tm = x_ref.shape[1]
    dh = q_out.shape[-1]
    hb = _rms(x_ref[0], g_ref[...], d_model).astype(BF16)
    qk = _dot(hb, wqk_ref[...])
    hw = n_heads * dh
    for hd in range(n_heads):
        lo = hd * dh
        q_out[0, hd] = _rms(qk[:, lo:lo + dh], qg_ref[...], dh).astype(BF16)
        k_h = _rms(qk[:, hw + lo:hw + lo + dh], kg_ref[...], dh)
        k_out[0, hd] = k_h.astype(BF16)
        vt_h = _dot_nt(wvt_ref[lo:lo + dh, :], hb)
        for j in range(tm // blk):
            vt_out[0, hd, j] = vt_h[:, j * blk:(j + 1) * blk].astype(BF16)
            km_out[0, j, hd:hd + 1, :] = jnp.mean(k_h[j * blk:(j + 1) * blk], axis=0, keepdims=True)


def _moba_proj(x, g, w_qk, w_vt, q_g, k_g, *, tm):
    B, S, D = x.shape
    H = MOBA_HEADS
    dh = q_g.shape[-1]
    blk = MOBA_BLOCK
    nb = S // blk

    def full(a):
        return pl.BlockSpec(a.shape, lambda b, i: (0,) * a.ndim)

    head = pl.BlockSpec((1, H, tm, dh), lambda b, i: (b, 0, i, 0))
    return pl.pallas_call(
        functools.partial(_moba_proj_kernel, blk=blk),
        grid=(B, S // tm),
        in_specs=[pl.BlockSpec((1, tm, D), lambda b, i: (b, i, 0)),
                  full(g), full(w_qk), full(w_vt), full(q_g), full(k_g)],
        out_specs=[head, head,
                   pl.BlockSpec((1, H, tm // blk, dh, blk), lambda b, i: (b, 0, i, 0, 0)),
                   pl.BlockSpec((1, tm // blk, H, dh), lambda b, i: (b, i, 0, 0))],
        out_shape=[jax.ShapeDtypeStruct((B, H, S, dh), BF16)] * 2
        + [jax.ShapeDtypeStruct((B, H, nb, dh, blk), BF16),
           jax.ShapeDtypeStruct((B, nb, H, dh), F32)],
        compiler_params=_params("parallel", "parallel"),
        name="moba_proj",
    )(x, g, w_qk, w_vt, q_g, k_g)


def _moba_attn_kernel(q_ref, k_ref, vt_ref, km_ref, pq_ref, pk_ref, tbl_ref, o_ref,
                      m_scr, l_scr, acc_scr, mask_scr, *, top_k, scale, n_buckets):
    n_heads, blk, dh = q_ref.shape[1], q_ref.shape[2], q_ref.shape[3]
    nb = km_ref.shape[2]
    own = pl.program_id(1)
    pq = pq_ref[0, 0]
    max_exact = n_buckets // 2
    log_ratio = math.log(REL_MAX_DISTANCE / max_exact)

    def bucket_of(n):
        pk = pk_ref[0, pl.ds(pl.multiple_of(n * blk, blk), blk), :]
        dist = jnp.maximum(pq - pk, 0)
        nf = jnp.maximum(dist, 1).astype(F32)
        large = max_exact + (jnp.log(nf / max_exact) / log_ratio
                             * (n_buckets - max_exact)).astype(jnp.int32)
        large = jnp.minimum(large, n_buckets - 1)
        return jnp.where(dist < max_exact, dist, large)

    def bias_of(hd, bucket):
        tbl = jnp.broadcast_to(tbl_ref[hd:hd + 1, :], (blk, LANES))
        parts = [jnp.take_along_axis(tbl, bucket[:, i * LANES:(i + 1) * LANES], axis=1,
                                     mode="promise_in_bounds") for i in range(blk // LANES)]
        return jnp.concatenate(parts, axis=1)

    def logits_t(hd, n, bucket):
        k_n = k_ref[0, hd, pl.ds(pl.multiple_of(n * blk, blk), blk), :]
        return _dot_nt(k_n, q_ref[0, hd]) * scale + bias_of(hd, bucket)

    bucket = bucket_of(own)
    key_i = lax.broadcasted_iota(jnp.int32, (blk, blk), 0)
    qry_i = lax.broadcasted_iota(jnp.int32, (blk, blk), 1)
    causal = key_i <= qry_i
    n_iota = lax.broadcasted_iota(jnp.int32, (nb, blk), 0)
    for hd in range(n_heads):
        s = jnp.where(causal, logits_t(hd, own, bucket), NEG_INF)
        m = jnp.max(s, axis=0, keepdims=True)
        p = jnp.exp(s - m)
        m_scr[hd] = m
        l_scr[hd] = jnp.sum(p, axis=0, keepdims=True)
        acc_scr[hd] = _dot(vt_ref[0, hd, own], p.astype(BF16))
        gate = _dot_nt(km_ref[0, hd], q_ref[0, hd].astype(F32), precision=lax.Precision.HIGHEST)
        cnt = jnp.zeros((nb, blk), F32)
        for mm in range(nb):
            g_m = gate[mm:mm + 1, :]
            above = (g_m > gate) | ((g_m == gate) & (mm < n_iota))
            cnt = cnt + jnp.where(above, jnp.where(mm < own, 1.0, 0.0), 0.0)
        keep = (cnt < top_k) & (n_iota < own)
        mask_scr[hd] = jnp.where(keep, 0.0, NEG_INF)

    def past_block(n, carry):
        bucket_n = bucket_of(n)
        for hd in range(n_heads):
            s = logits_t(hd, n, bucket_n) + mask_scr[hd, pl.ds(n, 1), :]
            m_old = m_scr[hd]
            m_new = jnp.maximum(m_old, jnp.max(s, axis=0, keepdims=True))
            alpha = jnp.exp(m_old - m_new)
            p = jnp.exp(s - m_new)
            l_scr[hd] = alpha * l_scr[hd] + jnp.sum(p, axis=0, keepdims=True)
            acc_scr[hd] = alpha * acc_scr[hd] + _dot(vt_ref[0, hd, n], p.astype(BF16))
            m_scr[hd] = m_new
        return carry

    lax.fori_loop(0, own, past_block, 0)
    for hd in range(n_heads):
        o_ref[0, :, hd * dh:(hd + 1) * dh] = (acc_scr[hd] / l_scr[hd]).T.astype(o_ref.dtype)


def _moba_attn(q, k, vt, k_mean, positions, table_h):
    B, H, S, dh = q.shape
    blk = MOBA_BLOCK
    nb = S // blk
    n_buckets = table_h.shape[1]
    tbl = jnp.pad(table_h, ((0, 0), (0, LANES - n_buckets)))
    pos_col = positions.reshape(B, S, 1)
    pos_blk = positions.reshape(B, nb, 1, blk)
    whole = lambda a: pl.BlockSpec((1,) + a.shape[1:], lambda b, c: (b,) + (0,) * (a.ndim - 1))
    return pl.pallas_call(
        functools.partial(_moba_attn_kernel, top_k=min(MOBA_TOP_K, nb - 1),
                          scale=1.0 / math.sqrt(dh), n_buckets=n_buckets),
        grid=(B, nb),
        in_specs=[pl.BlockSpec((1, H, blk, dh), lambda b, c: (b, 0, c, 0)),
                  whole(k), whole(vt), whole(k_mean),
                  pl.BlockSpec((1, 1, 1, blk), lambda b, c: (b, c, 0, 0)),
                  whole(pos_col),
                  pl.BlockSpec(tbl.shape, lambda b, c: (0, 0))],
        out_specs=pl.BlockSpec((1, blk, H * dh), lambda b, c: (b, c, 0)),
        out_shape=jax.ShapeDtypeStruct((B, S, H * dh), BF16),
        scratch_shapes=[pltpu.VMEM((H, 1, blk), F32), pltpu.VMEM((H, 1, blk), F32),
                        pltpu.VMEM((H, dh, blk), F32), pltpu.VMEM((H, nb, blk), F32)],
        compiler_params=_params("parallel", "arbitrary"),
        name="moba_attn",
    )(q, k, vt, k_mean, pos_blk, pos_col, tbl)


def _oproj_mlp_kernel(x_ref, a_ref, wo_ref, g_ref, win_ref, wout_ref, o_ref, h_scr):
    @pl.when(pl.program_id(1) == 0)
    def _():
        x1 = x_ref[...] + _dot(a_ref[...], wo_ref[...])
        o_ref[...] = x1
        h_scr[...] = _rms(x1, g_ref[...], x1.shape[-1]).astype(BF16)

    a = jnp.maximum(_dot(h_scr[...], win_ref[...]), 0.0)
    o_ref[...] += _dot((a * a).astype(BF16), wout_ref[...])


def _oproj_mlp(x, attn, w_o, g, w_in, w_out, *, tm, tf):
    T, D = x.shape
    F = w_in.shape[1]
    return pl.pallas_call(
        _oproj_mlp_kernel,
        grid=(T // tm, F // tf),
        in_specs=[pl.BlockSpec((tm, D), lambda i, j: (i, 0)),
                  pl.BlockSpec((tm, attn.shape[1]), lambda i, j: (i, 0)),
                  pl.BlockSpec(w_o.shape, lambda i, j: (0, 0)),
                  pl.BlockSpec(g.shape, lambda i, j: (0, 0)),
                  pl.BlockSpec((D, tf), lambda i, j: (0, j)),
                  pl.BlockSpec((tf, D), lambda i, j: (j, 0))],
        out_specs=pl.BlockSpec((tm, D), lambda i, j: (i, 0)),
        out_shape=jax.ShapeDtypeStruct((T, D), F32),
        scratch_shapes=[pltpu.VMEM((tm, D), BF16)],
        compiler_params=_params("parallel", "arbitrary"),
        name="oproj_mlp",
    )(x, attn, w_o, g, w_in, w_out)


def _gather_cols(w, src):
    src = np.asarray(src)
    cols = jnp.take(w, jnp.asarray(np.maximum(src, 0)), axis=-1)
    return jnp.where(jnp.asarray(src >= 0), cols, jnp.zeros((), w.dtype))


def _rope_group_src(base):
    half = MLA_ROPE_DIM // 2
    src = np.full((LANES,), -1, np.int64)
    src[:half] = base + np.arange(half)
    src[LANES // 2:LANES // 2 + half] = base + half + np.arange(half)
    return src


def _mla_layouts(q_rank, kv_rank):
    H = MLA_HEADS
    qk = MLA_NOPE_DIM + MLA_ROPE_DIM
    kvw = MLA_NOPE_DIM + MLA_V_DIM
    in_src = np.concatenate([np.arange(q_rank + kv_rank), _rope_group_src(q_rank + kv_rank)])
    uq_src = np.concatenate(
        [h * qk + np.arange(MLA_NOPE_DIM) for h in range(H)]
        + [_rope_group_src(h * qk + MLA_NOPE_DIM) for h in range(H)])
    ukv_src = np.concatenate(
        [h * kvw + np.arange(MLA_NOPE_DIM) for h in range(H)]
        + [h * kvw + MLA_NOPE_DIM + np.arange(MLA_V_DIM) for h in range(H)])
    return in_src, uq_src, ukv_src, _rope_group_src(0)


def kernel(x, positions, rel_bias_table, attn_norm, mlp_norm, mla_w_in, mla_q_a_norm, mla_kv_a_norm,
           mla_w_uq, mla_w_ukv, mla_q_nope_norm, mla_q_rope_norm, mla_k_nope_norm, mla_k_rope_norm,
           mla_w_o, moba_w_qkv, moba_q_norm, moba_k_norm, moba_w_o, mlp_w_in, mlp_w_out):
    B, S, D = x.shape
    depth = attn_norm.shape[0]
    q_rank = mla_q_a_norm.shape[-1]
    kv_rank = mla_kv_a_norm.shape[-1]
    assert S % MOBA_BLOCK == 0 and mla_w_in.shape[-1] == q_rank + kv_rank + MLA_ROPE_DIM
    in_src, uq_src, ukv_src, rope_src = _mla_layouts(q_rank, kv_rank)
    row = lambda v: v.reshape(1, -1)
    cos_t, sin_t = _rope_tables(positions)
    table_h = rel_bias_table.T
    for i in range(depth):
        j = i // 2
        if i % 2 == 0:
            q, k, v = _mla_proj(
                x, row(attn_norm[i]), _gather_cols(mla_w_in[j], in_src).astype(BF16),
                row(mla_q_a_norm[j]), row(mla_kv_a_norm[j]),
                _gather_cols(mla_w_uq[j], uq_src).astype(BF16),
                _gather_cols(mla_w_ukv[j], ukv_src).astype(BF16),
                row(mla_q_nope_norm[j]), row(_gather_cols(mla_q_rope_norm[j], rope_src)),
                row(mla_k_nope_norm[j]), row(_gather_cols(mla_k_rope_norm[j], rope_src)),
                cos_t, sin_t, tm=512)
            attn = _mla_attn(q, k, v, tq=256)
            w_o = mla_w_o[j]
        else:
            hw = MOBA_HEADS * moba_q_norm.shape[-1]
            q, k, vt, k_mean = _moba_proj(x, row(attn_norm[i]), moba_w_qkv[j][:, :2 * hw].astype(BF16),
                                          moba_w_qkv[j][:, 2 * hw:].T.astype(BF16),
                                          row(moba_q_norm[j]), row(moba_k_norm[j]), tm=512)
            attn = _moba_attn(q, k, vt, k_mean.transpose(0, 2, 1, 3), positions, table_h)
            w_o = moba_w_o[j]
        x = _oproj_mlp(x.reshape(B * S, D), attn.reshape(B * S, -1), w_o.astype(BF16),
                       row(mlp_norm[i]), mlp_w_in[i].astype(BF16), mlp_w_out[i].astype(BF16),
                       tm=1024, tf=512).reshape(B, S, D)
    return x
```

```python
import functools
import math

import numpy as np
import jax
import jax.numpy as jnp
from jax import lax
from jax.experimental import pallas as pl
from jax.experimental.pallas import tpu as pltpu

MLA_HEADS = 8
MLA_NOPE_DIM = 128
MLA_ROPE_DIM = 64
MLA_V_DIM = 128
ROPE_THETA = 10000.0
MOBA_HEADS = 8
MOBA_BLOCK = 256
MOBA_TOP_K = 3
REL_MAX_DISTANCE = 1024
RMS_EPS = 1e-6

LANES = 128
VMEM_LIMIT_BYTES = 56 * 1024 * 1024
PROLOGUE_CHUNKS = 4

F32 = jnp.float32
BF16 = jnp.bfloat16
NEG_INF = float("-inf")
LOG2E = math.log2(math.e)


def _dot(a, b):
    return jnp.dot(a, b, preferred_element_type=F32)


def _dot_nt(a, b, precision=None):
    return lax.dot_general(a, b, (((1,), (1,)), ((), ())),
                           preferred_element_type=F32, precision=precision)


def _lane_gather(tbl, idx):
    dn = lax.GatherDimensionNumbers(offset_dims=(), collapsed_slice_dims=(1,), start_index_map=(1,),
                                    operand_batching_dims=(0,), start_indices_batching_dims=(0,))
    return lax.gather(tbl, idx[..., None], dn, slice_sizes=(1, 1),
                      mode=lax.GatherScatterMode.PROMISE_IN_BOUNDS)


def _rms(x, g, n):
    ms = jnp.sum(x * x, axis=-1, keepdims=True) * (1.0 / n)
    return x * lax.rsqrt(ms + RMS_EPS) * g


def _params(*sem):
    return pltpu.CompilerParams(dimension_semantics=sem, vmem_limit_bytes=VMEM_LIMIT_BYTES)


def _rope_table_kernel(pos_ref, freq_ref, cos_ref, sin_ref):
    ang = pos_ref[...].astype(F32) * freq_ref[...]
    cos_ref[...] = jnp.cos(ang)
    sin_ref[...] = jnp.sin(ang)


def _rope_tables(positions):
    B, S = positions.shape
    half = MLA_ROPE_DIM // 2
    per_row = LANES // half
    inv_freq = ROPE_THETA ** (-jnp.arange(0, MLA_ROPE_DIM, 2, dtype=F32) / MLA_ROPE_DIM)
    rows = B * S // per_row
    pos_rep = jnp.repeat(positions.reshape(rows, per_row), half, axis=1)
    freq = jnp.tile(inv_freq, per_row).reshape(1, LANES)
    tr = min(rows, 1024)
    cos, sin = pl.pallas_call(
        _rope_table_kernel,
        grid=(rows // tr,),
        in_specs=[pl.BlockSpec((tr, LANES), lambda i: (i, 0)),
                  pl.BlockSpec((1, LANES), lambda i: (0, 0))],
        out_specs=[pl.BlockSpec((tr, LANES), lambda i: (i, 0))] * 2,
        out_shape=[jax.ShapeDtypeStruct((rows, LANES), F32)] * 2,
        compiler_params=_params("parallel"),
        name="rope_tables",
    )(pos_rep, freq)
    cos = cos.reshape(B, S, half)
    sin = sin.reshape(B, S, half)
    z = jnp.zeros_like(cos)
    cos_t = jnp.concatenate([cos, z, cos, z], axis=-1)
    sin_t = jnp.concatenate([-sin, z, sin, z], axis=-1)
    return cos_t, sin_t


def _mla_proj_kernel(x_ref, g_ref, w_in_ref, qa_g_ref, kva_g_ref, w_uq_ref, w_uk_ref, w_uvt_ref,
                     qn_g_ref, qr_g_ref, kn_g_ref, kr_g_ref, cos_ref, sin_ref,
                     q_out, k_out, vt_out, *, q_rank, kv_rank):
    d_model = x_ref.shape[-1]
    n_heads = q_out.shape[1]
    h = _rms(x_ref[0], g_ref[...], d_model)
    proj = _dot(h.astype(BF16), w_in_ref[...])
    c_q = proj[:, :q_rank]
    c_kv = proj[:, q_rank:q_rank + kv_rank]
    k_pe = proj[:, q_rank + kv_rank:]
    qf = _dot(_rms(c_q, qa_g_ref[...], q_rank).astype(BF16), w_uq_ref[...])
    ckv_n = _rms(c_kv, kva_g_ref[...], kv_rank).astype(BF16)
    kf = _dot(ckv_n, w_uk_ref[...])
    cos = cos_ref[0]
    sin = sin_ref[0]

    def rope(t):
        return t * cos + pltpu.roll(t, LANES // 2, axis=1) * sin

    k_pe = rope(_rms(k_pe, kr_g_ref[...], MLA_ROPE_DIM)).astype(BF16)
    nope_w = n_heads * MLA_NOPE_DIM
    for hd in range(n_heads):
        lo = hd * LANES
        q_n = _rms(qf[:, lo:lo + LANES], qn_g_ref[...], MLA_NOPE_DIM)
        q_p = rope(_rms(qf[:, nope_w + lo:nope_w + lo + LANES], qr_g_ref[...], MLA_ROPE_DIM))
        q_out[0, hd] = jnp.concatenate([q_n.astype(BF16), q_p.astype(BF16)], axis=-1)
        k_n = _rms(kf[:, lo:lo + LANES], kn_g_ref[...], MLA_NOPE_DIM)
        k_out[0, hd] = jnp.concatenate([k_n.astype(BF16), k_pe], axis=-1)
        vt_out[0, hd] = _dot_nt(w_uvt_ref[hd * MLA_V_DIM:(hd + 1) * MLA_V_DIM, :], ckv_n).astype(BF16)


def _mla_proj(x, g, w_in, qa_g, kva_g, w_uq, w_uk, w_uvt, qn_g, qr_g, kn_g, kr_g, cos_t, sin_t, *, tm):
    B, S, D = x.shape
    H = MLA_HEADS
    q_rank = qa_g.shape[-1]
    kv_rank = kva_g.shape[-1]
    qk_w = 2 * LANES

    def full(a):
        return pl.BlockSpec(a.shape, lambda b, i: (0,) * a.ndim)

    tok = lambda w: pl.BlockSpec((1, tm, w), lambda b, i: (b, i, 0))
    head = lambda w: pl.BlockSpec((1, H, tm, w), lambda b, i: (b, 0, i, 0))
    return pl.pallas_call(
        functools.partial(_mla_proj_kernel, q_rank=q_rank, kv_rank=kv_rank),
        grid=(B, S // tm),
        in_specs=[tok(D), full(g), full(w_in), full(qa_g), full(kva_g), full(w_uq), full(w_uk), full(w_uvt),
                  full(qn_g), full(qr_g), full(kn_g), full(kr_g), tok(LANES), tok(LANES)],
        out_specs=[head(qk_w), head(qk_w),
                   pl.BlockSpec((1, H, MLA_V_DIM, tm), lambda b, i: (b, 0, 0, i))],
        out_shape=[jax.ShapeDtypeStruct((B, H, S, qk_w), BF16),
                   jax.ShapeDtypeStruct((B, H, S, qk_w), BF16),
                   jax.ShapeDtypeStruct((B, H, MLA_V_DIM, S), BF16)],
        compiler_params=_params("parallel", "parallel"),
        name="mla_proj",
    )(x, g, w_in, qa_g, kva_g, w_uq, w_uk, w_uvt, qn_g, qr_g, kn_g, kr_g, cos_t, sin_t)


def _mla_attn_kernel(q_ref, k_ref, vt_ref, o_ref, s_scr, p_scr, *, tq, scale):
    S = q_ref.shape[2]
    key_i = lax.broadcasted_iota(jnp.int32, (tq, tq), 0)
    qry_i = lax.broadcasted_iota(jnp.int32, (tq, tq), 1)
    causal = key_i <= qry_i
    for i in range(S // tq):
        lo, hi, slot = i * tq, (i + 1) * tq, i % 2
        q = q_ref[0, 0, lo:hi, :]
        if i > 0:
            s_scr[slot, :lo, :] = _dot_nt(k_ref[0, 0, :lo, :], q) * (scale * LOG2E)
        s_d = _dot_nt(k_ref[0, 0, lo:hi, :], q) * (scale * LOG2E)
        s_scr[slot, lo:hi, :] = jnp.where(causal, s_d, NEG_INF)
        s = s_scr[slot, :hi, :]
        p = jnp.exp2(s - jnp.max(s, axis=0, keepdims=True))
        l = jnp.sum(p, axis=0, keepdims=True)
        p_scr[slot, :hi, :] = p.astype(BF16)
        acc = _dot(vt_ref[0, 0, :, :hi], p_scr[slot, :hi, :])
        o_ref[0, lo:hi, :] = (acc / l).T.astype(o_ref.dtype)


def _mla_attn(q, k, vt, *, tq):
    B, H, S, qk_w = q.shape
    dv = vt.shape[2]
    scale = 1.0 / math.sqrt(MLA_NOPE_DIM + MLA_ROPE_DIM)
    spec = lambda w: pl.BlockSpec((1, 1, S, w), lambda b, h: (b, h, 0, 0))
    return pl.pallas_call(
        functools.partial(_mla_attn_kernel, tq=tq, scale=scale),
        grid=(B, H),
        in_specs=[spec(qk_w), spec(qk_w), pl.BlockSpec((1, 1, dv, S), lambda b, h: (b, h, 0, 0))],
        out_specs=pl.BlockSpec((1, S, dv), lambda b, h: (b, 0, h)),
        out_shape=jax.ShapeDtypeStruct((B, S, H * dv), BF16),
        scratch_shapes=[pltpu.VMEM((2, S, tq), F32), pltpu.VMEM((2, S, tq), BF16)],
        compiler_params=_params("parallel", "parallel"),
        name="mla_attn",
    )(q, k, vt)


def _moba_proj_kernel(x_ref, g_ref, wqk_ref, wvt_ref, qg_ref, kg_ref, q_out, k_out, vt_out, km_out, *, blk):
    d_model = x_ref.shape[-1]
    n_heads = q_out.shape[1]
    tm = x_ref.shape[1]
    dh = q_out.shape[-1]
    hb = _rms(x_ref[0], g_ref[...], d_model).astype(BF16)
    qk = _dot(hb, wqk_ref[...])
    hw = n_heads * dh
    for hd in range(n_heads):
        lo = hd * dh
        q_out[0, hd] = _rms(qk[:, lo:lo + dh], qg_ref[...], dh).astype(BF16)
        k_h = _rms(qk[:, hw + lo:hw + lo + dh], kg_ref[...], dh)
        k_out[0, hd] = k_h.astype(BF16)
        vt_h = _dot_nt(wvt_ref[lo:lo + dh, :], hb)
        for j in range(tm // blk):
            vt_out[0, hd, j] = vt_h[:, j * blk:(j + 1) * blk].astype(BF16)
            km_out[0, j, hd:hd + 1, :] = jnp.mean(k_h[j * blk:(j + 1) * blk], axis=0, keepdims=True)


def _moba_proj(x, g, w_qk, w_vt, q_g, k_g, *, tm):
    B, S, D = x.shape
    H = MOBA_HEADS
    dh = q_g.shape[-1]
    blk = MOBA_BLOCK
    nb = S // blk

    def full(a):
        return pl.BlockSpec(a.shape, lambda b, i: (0,) * a.ndim)

    head = pl.BlockSpec((1, H, tm, dh), lambda b, i: (b, 0, i, 0))
    return pl.pallas_call(
        functools.partial(_moba_proj_kernel, blk=blk),
        grid=(B, S // tm),
        in_specs=[pl.BlockSpec((1, tm, D), lambda b, i: (b, i, 0)),
                  full(g), full(w_qk), full(w_vt), full(q_g), full(k_g)],
        out_specs=[head, head,
                   pl.BlockSpec((1, H, tm // blk, dh, blk), lambda b, i: (b, 0, i, 0, 0)),
                   pl.BlockSpec((1, tm // blk, H, dh), lambda b, i: (b, i, 0, 0))],
        out_shape=[jax.ShapeDtypeStruct((B, H, S, dh), BF16)] * 2
        + [jax.ShapeDtypeStruct((B, H, nb, dh, blk), BF16),
           jax.ShapeDtypeStruct((B, nb, H, dh), F32)],
        compiler_params=_params("parallel", "parallel"),
        name="moba_proj",
    )(x, g, w_qk, w_vt, q_g, k_g)


def _moba_attn_kernel(q_ref, k_ref, vt_ref, km_ref, pq_ref, pk_ref, tbl_ref, o_ref,
                      m_scr, l_scr, acc_scr, mask_scr, al_scr, bkt_scr, s_scr, p_scr,
                      *, top_k, scale, n_buckets):
    n_heads, blk, dh = q_ref.shape[1], q_ref.shape[2], q_ref.shape[3]
    nb = km_ref.shape[2]
    own = pl.program_id(1)
    pq = pq_ref[0, 0]
    max_exact = n_buckets // 2
    log_ratio = math.log(REL_MAX_DISTANCE / max_exact)
    tbl2 = tbl_ref[...] * LOG2E

    def stage_bucket(n):
        pk = pk_ref[0, pl.ds(pl.multiple_of(n * blk, blk), blk), :]
        dist = jnp.maximum(pq - pk, 0)
        nf = jnp.maximum(dist, 1).astype(F32)
        large = max_exact + (jnp.log(nf / max_exact) / log_ratio
                             * (n_buckets - max_exact)).astype(jnp.int32)
        large = jnp.minimum(large, n_buckets - 1)
        bkt_scr[...] = jnp.where(dist < max_exact, dist, large)

    def scores(hd, n, causal=None):
        k_n = k_ref[0, hd, pl.ds(pl.multiple_of(n * blk, blk), blk), :]
        tbl = jnp.broadcast_to(tbl2[hd:hd + 1, :], (blk, LANES))
        bias = jnp.concatenate([_lane_gather(tbl, bkt_scr[:, i * LANES:(i + 1) * LANES])
                                for i in range(blk // LANES)], axis=1)
        s = _dot_nt(k_n, q_ref[0, hd]) * (scale * LOG2E) + bias
        s_scr[hd] = s if causal is None else jnp.where(causal, s, NEG_INF)

    stage_bucket(own)
    key_i = lax.broadcasted_iota(jnp.int32, (blk, blk), 0)
    qry_i = lax.broadcasted_iota(jnp.int32, (blk, blk), 1)
    causal = key_i <= qry_i
    for hd in range(n_heads):
        scores(hd, own, causal)
    for hd in range(n_heads):
        s = s_scr[hd]
        m = jnp.max(s, axis=0, keepdims=True)
        p = jnp.exp2(s - m)
        m_scr[hd] = m
        l_scr[hd] = jnp.sum(p, axis=0, keepdims=True)
        p_scr[hd] = p.astype(BF16)
    for hd in range(n_heads):
        acc_scr[hd] = _dot(vt_ref[0, hd, own], p_scr[hd])

    n_iota = lax.broadcasted_iota(jnp.int32, (nb, blk), 0)
    for hd in range(n_heads):
        gate = _dot_nt(km_ref[0, hd], q_ref[0, hd].astype(F32), precision=lax.Precision.HIGHEST)
        cnt = jnp.zeros((nb, blk), F32)
        for mm in range(nb):
            g_m = gate[mm:mm + 1, :]
            above = (g_m > gate) | ((g_m == gate) & (mm < n_iota))
            cnt = cnt + jnp.where(above, jnp.where(mm < own, 1.0, 0.0), 0.0)
        keep = (cnt < top_k) & (n_iota < own)
        mask_scr[hd] = jnp.where(keep, 0.0, NEG_INF)

    def past_block(n, carry):
        stage_bucket(n)
        for hd in range(n_heads):
            scores(hd, n)
        for hd in range(n_heads):
            s = s_scr[hd]
            mask = mask_scr[hd, pl.ds(n, 1), :]
            m_old = m_scr[hd]
            m_new = jnp.maximum(m_old, jnp.max(s, axis=0, keepdims=True) + mask)
            alpha = jnp.exp2(m_old - m_new)
            p = jnp.exp2(s - (m_new - mask))
            l_scr[hd] = alpha * l_scr[hd] + jnp.sum(p, axis=0, keepdims=True)
            p_scr[hd] = p.astype(BF16)
            m_scr[hd] = m_new
            al_scr[hd] = alpha
        for hd in range(n_heads):
            acc_scr[hd] = al_scr[hd] * acc_scr[hd] + _dot(vt_ref[0, hd, n], p_scr[hd])
        return carry

    lax.fori_loop(0, own, past_block, 0)
    for hd in range(n_heads):
        o_ref[0, :, hd * dh:(hd + 1) * dh] = (acc_scr[hd] / l_scr[hd]).T.astype(o_ref.dtype)


def _moba_attn(q, k, vt, k_mean, positions, table_h):
    B, H, S, dh = q.shape
    blk = MOBA_BLOCK
    nb = S // blk
    n_buckets = table_h.shape[1]
    tbl = jnp.pad(table_h, ((0, 0), (0, LANES - n_buckets)))
    pos_col = positions.reshape(B, S, 1)
    pos_blk = positions.reshape(B, nb, 1, blk)
    whole = lambda a: pl.BlockSpec((1,) + a.shape[1:], lambda b, c: (b,) + (0,) * (a.ndim - 1))
    return pl.pallas_call(
        functools.partial(_moba_attn_kernel, top_k=min(MOBA_TOP_K, nb - 1),
                          scale=1.0 / math.sqrt(dh), n_buckets=n_buckets),
        grid=(B, nb),
        in_specs=[pl.BlockSpec((1, H, blk, dh), lambda b, c: (b, 0, c, 0)),
                  whole(k), whole(vt), whole(k_mean),
                  pl.BlockSpec((1, 1, 1, blk), lambda b, c: (b, c, 0, 0)),
                  whole(pos_col),
                  pl.BlockSpec(tbl.shape, lambda b, c: (0, 0))],
        out_specs=pl.BlockSpec((1, blk, H * dh), lambda b, c: (b, c, 0)),
        out_shape=jax.ShapeDtypeStruct((B, S, H * dh), BF16),
        scratch_shapes=[pltpu.VMEM((H, 1, blk), F32), pltpu.VMEM((H, 1, blk), F32),
                        pltpu.VMEM((H, dh, blk), F32), pltpu.VMEM((H, nb, blk), F32),
                        pltpu.VMEM((H, 1, blk), F32), pltpu.VMEM((blk, blk), jnp.int32),
                        pltpu.VMEM((H, blk, blk), F32), pltpu.VMEM((H, blk, blk), BF16)],
        compiler_params=_params("parallel", "arbitrary"),
        name="moba_attn",
    )(q, k, vt, k_mean, pos_blk, pos_col, tbl)


def _oproj_mlp_kernel(x_ref, a_ref, wo_ref, g_ref, win_ref, wout_ref, o_ref, h_scr):
    @pl.when(pl.program_id(1) == 0)
    def _():
        rows = x_ref.shape[0] // PROLOGUE_CHUNKS
        for r in range(PROLOGUE_CHUNKS):
            sl = slice(r * rows, (r + 1) * rows)
            x1 = x_ref[sl, :] + _dot(a_ref[sl, :], wo_ref[...])
            o_ref[sl, :] = x1
            h_scr[sl, :] = _rms(x1, g_ref[...], x1.shape[-1]).astype(BF16)

    a = jnp.maximum(_dot(h_scr[...], win_ref[...]), 0.0)
    o_ref[...] += _dot((a * a).astype(BF16), wout_ref[...])


def _oproj_mlp(x, attn, w_o, g, w_in, w_out, *, tm, tf):
    T, D = x.shape
    F = w_in.shape[1]
    return pl.pallas_call(
        _oproj_mlp_kernel,
        grid=(T // tm, F // tf),
        in_specs=[pl.BlockSpec((tm, D), lambda i, j: (i, 0)),
                  pl.BlockSpec((tm, attn.shape[1]), lambda i, j: (i, 0)),
                  pl.BlockSpec(w_o.shape, lambda i, j: (0, 0)),
                  pl.BlockSpec(g.shape, lambda i, j: (0, 0)),
                  pl.BlockSpec((D, tf), lambda i, j: (0, j)),
                  pl.BlockSpec((tf, D), lambda i, j: (j, 0))],
        out_specs=pl.BlockSpec((tm, D), lambda i, j: (i, 0)),
        out_shape=jax.ShapeDtypeStruct((T, D), F32),
        scratch_shapes=[pltpu.VMEM((tm, D), BF16)],
        compiler_params=_params("parallel", "arbitrary"),
        name="oproj_mlp",
    )(x, attn, w_o, g, w_in, w_out)


def _gather_cols(w, src):
    src = np.asarray(src)
    cols = jnp.take(w, jnp.asarray(np.maximum(src, 0)), axis=-1)
    return jnp.where(jnp.asarray(src >= 0), cols, jnp.zeros((), w.dtype))


def _rope_group_src(base):
    half = MLA_ROPE_DIM // 2
    src = np.full((LANES,), -1, np.int64)
    src[:half] = base + np.arange(half)
    src[LANES // 2:LANES // 2 + half] = base + half + np.arange(half)
    return src


def _mla_layouts(q_rank, kv_rank):
    H = MLA_HEADS
    qk = MLA_NOPE_DIM + MLA_ROPE_DIM
    kvw = MLA_NOPE_DIM + MLA_V_DIM
    in_src = np.concatenate([np.arange(q_rank + kv_rank), _rope_group_src(q_rank + kv_rank)])
    uq_src = np.concatenate(
        [h * qk + np.arange(MLA_NOPE_DIM) for h in range(H)]
        + [_rope_group_src(h * qk + MLA_NOPE_DIM) for h in range(H)])
    uk_src = np.concatenate([h * kvw + np.arange(MLA_NOPE_DIM) for h in range(H)])
    uv_src = np.concatenate([h * kvw + MLA_NOPE_DIM + np.arange(MLA_V_DIM) for h in range(H)])
    return in_src, uq_src, uk_src, uv_src, _rope_group_src(0)


def kernel(x, positions, rel_bias_table, attn_norm, mlp_norm, mla_w_in, mla_q_a_norm, mla_kv_a_norm,
           mla_w_uq, mla_w_ukv, mla_q_nope_norm, mla_q_rope_norm, mla_k_nope_norm, mla_k_rope_norm,
           mla_w_o, moba_w_qkv, moba_q_norm, moba_k_norm, moba_w_o, mlp_w_in, mlp_w_out):
    B, S, D = x.shape
    depth = attn_norm.shape[0]
    q_rank = mla_q_a_norm.shape[-1]
    kv_rank = mla_kv_a_norm.shape[-1]
    assert S % MOBA_BLOCK == 0 and mla_w_in.shape[-1] == q_rank + kv_rank + MLA_ROPE_DIM
    in_src, uq_src, uk_src, uv_src, rope_src = _mla_layouts(q_rank, kv_rank)
    row = lambda v: v.reshape(1, -1)
    cos_t, sin_t = _rope_tables(positions)
    table_h = rel_bias_table.T
    for i in range(depth):
        j = i // 2
        if i % 2 == 0:
            q, k, vt = _mla_proj(
                x, row(attn_norm[i]), _gather_cols(mla_w_in[j], in_src).astype(BF16),
                row(mla_q_a_norm[j]), row(mla_kv_a_norm[j]),
                _gather_cols(mla_w_uq[j], uq_src).astype(BF16),
                _gather_cols(mla_w_ukv[j], uk_src).astype(BF16),
                _gather_cols(mla_w_ukv[j], uv_src).T.astype(BF16),
                row(mla_q_nope_norm[j]), row(_gather_cols(mla_q_rope_norm[j], rope_src)),
                row(mla_k_nope_norm[j]), row(_gather_cols(mla_k_rope_norm[j], rope_src)),
                cos_t, sin_t, tm=512)
            attn = _mla_attn(q, k, vt, tq=256)
            w_o = mla_w_o[j]
        else:
            hw = MOBA_HEADS * moba_q_norm.shape[-1]
            q, k, vt, k_mean = _moba_proj(x, row(attn_norm[i]), moba_w_qkv[j][:, :2 * hw].astype(BF16),
                                          moba_w_qkv[j][:, 2 * hw:].T.astype(BF16),
                                          row(moba_q_norm[j]), row(moba_k_norm[j]), tm=512)
            attn = _moba_attn(q, k, vt, k_mean.transpose(0, 2, 1, 3), positions, table_h)
            w_o = moba_w_o[j]
        x = _oproj_mlp(x.reshape(B * S, D), attn.reshape(B * S, -1), w_o.astype(BF16),
                       row(mlp_norm[i]), mlp_w_in[i].astype(BF16), mlp_w_out[i].astype(BF16),
                       tm=1024, tf=1024).reshape(B, S, D)
    return x
```

```python
import functools
import math

import numpy as np
import jax
import jax.numpy as jnp
from jax import lax
from jax.experimental import pallas as pl
from jax.experimental.pallas import tpu as pltpu

MLA_HEADS = 8
MLA_NOPE_DIM = 128
MLA_ROPE_DIM = 64
MLA_V_DIM = 128
ROPE_THETA = 10000.0
MOBA_HEADS = 8
MOBA_BLOCK = 256
MOBA_TOP_K = 3
REL_MAX_DISTANCE = 1024
RMS_EPS = 1e-6

LANES = 128
VMEM_LIMIT_BYTES = 56 * 1024 * 1024
PROLOGUE_CHUNKS = 4
PROJ_CHUNKS = 2

F32 = jnp.float32
BF16 = jnp.bfloat16
NEG_INF = float("-inf")
LOG2E = math.log2(math.e)


def _dot(a, b):
    return jnp.dot(a, b, preferred_element_type=F32)


def _dot_nt(a, b, precision=None):
    return lax.dot_general(a, b, (((1,), (1,)), ((), ())),
                           preferred_element_type=F32, precision=precision)


def _lane_gather(tbl, idx):
    dn = lax.GatherDimensionNumbers(offset_dims=(), collapsed_slice_dims=(1,), start_index_map=(1,),
                                    operand_batching_dims=(0,), start_indices_batching_dims=(0,))
    return lax.gather(tbl, idx[..., None], dn, slice_sizes=(1, 1),
                      mode=lax.GatherScatterMode.PROMISE_IN_BOUNDS)


def _rms(x, g, n):
    ms = jnp.sum(x * x, axis=-1, keepdims=True) * (1.0 / n)
    return x * lax.rsqrt(ms + RMS_EPS) * g


def _params(*sem):
    return pltpu.CompilerParams(dimension_semantics=sem, vmem_limit_bytes=VMEM_LIMIT_BYTES)


def _rope_table_kernel(pos_ref, freq_ref, cos_ref, sin_ref):
    ang = pos_ref[...].astype(F32) * freq_ref[...]
    cos_ref[...] = jnp.cos(ang)
    sin_ref[...] = jnp.sin(ang)


def _rope_tables(positions):
    B, S = positions.shape
    half = MLA_ROPE_DIM // 2
    per_row = LANES // half
    inv_freq = ROPE_THETA ** (-jnp.arange(0, MLA_ROPE_DIM, 2, dtype=F32) / MLA_ROPE_DIM)
    rows = B * S // per_row
    pos_rep = jnp.repeat(positions.reshape(rows, per_row), half, axis=1)
    freq = jnp.tile(inv_freq, per_row).reshape(1, LANES)
    tr = min(rows, 1024)
    cos, sin = pl.pallas_call(
        _rope_table_kernel,
        grid=(rows // tr,),
        in_specs=[pl.BlockSpec((tr, LANES), lambda i: (i, 0)),
                  pl.BlockSpec((1, LANES), lambda i: (0, 0))],
        out_specs=[pl.BlockSpec((tr, LANES), lambda i: (i, 0))] * 2,
        out_shape=[jax.ShapeDtypeStruct((rows, LANES), F32)] * 2,
        compiler_params=_params("parallel"),
        name="rope_tables",
    )(pos_rep, freq)
    cos = cos.reshape(B, S, half)
    sin = sin.reshape(B, S, half)
    z = jnp.zeros_like(cos)
    cos_t = jnp.concatenate([cos, z, cos, z], axis=-1)
    sin_t = jnp.concatenate([-sin, z, sin, z], axis=-1)
    return cos_t, sin_t


def _mla_proj_kernel(x_ref, g_ref, w_in_ref, qa_g_ref, kva_g_ref, w_uq_ref, w_uk_ref, w_uvt_ref,
                     qn_g_ref, qr_g_ref, kn_g_ref, kr_g_ref, cos_ref, sin_ref,
                     q_out, k_out, vt_out, *, q_rank, kv_rank):
    d_model = x_ref.shape[-1]
    n_heads = q_out.shape[1]
    nope_w = n_heads * MLA_NOPE_DIM
    rows = x_ref.shape[1] // PROJ_CHUNKS
    for r in range(PROJ_CHUNKS):
        sl = slice(r * rows, (r + 1) * rows)
        h = _rms(x_ref[0, sl, :], g_ref[...], d_model)
        proj = _dot(h.astype(BF16), w_in_ref[...])
        c_q = proj[:, :q_rank]
        c_kv = proj[:, q_rank:q_rank + kv_rank]
        k_pe = proj[:, q_rank + kv_rank:]
        qf = _dot(_rms(c_q, qa_g_ref[...], q_rank).astype(BF16), w_uq_ref[...])
        ckv_n = _rms(c_kv, kva_g_ref[...], kv_rank).astype(BF16)
        kf = _dot(ckv_n, w_uk_ref[...])
        cos = cos_ref[0, sl, :]
        sin = sin_ref[0, sl, :]

        def rope(t):
            return t * cos + pltpu.roll(t, LANES // 2, axis=1) * sin

        k_pe = rope(_rms(k_pe, kr_g_ref[...], MLA_ROPE_DIM)).astype(BF16)
        for hd in range(n_heads):
            lo = hd * LANES
            q_n = _rms(qf[:, lo:lo + LANES], qn_g_ref[...], MLA_NOPE_DIM)
            q_p = rope(_rms(qf[:, nope_w + lo:nope_w + lo + LANES], qr_g_ref[...], MLA_ROPE_DIM))
            q_out[0, hd, sl, :] = jnp.concatenate([q_n.astype(BF16), q_p.astype(BF16)], axis=-1)
            k_n = _rms(kf[:, lo:lo + LANES], kn_g_ref[...], MLA_NOPE_DIM)
            k_out[0, hd, sl, :] = jnp.concatenate([k_n.astype(BF16), k_pe], axis=-1)
            vt_out[0, hd, :, sl] = _dot_nt(w_uvt_ref[hd * MLA_V_DIM:(hd + 1) * MLA_V_DIM, :], ckv_n).astype(BF16)


def _mla_proj(x, g, w_in, qa_g, kva_g, w_uq, w_uk, w_uvt, qn_g, qr_g, kn_g, kr_g, cos_t, sin_t, *, tm):
    B, S, D = x.shape
    H = MLA_HEADS
    q_rank = qa_g.shape[-1]
    kv_rank = kva_g.shape[-1]
    qk_w = 2 * LANES

    def full(a):
        return pl.BlockSpec(a.shape, lambda b, i: (0,) * a.ndim)

    tok = lambda w: pl.BlockSpec((1, tm, w), lambda b, i: (b, i, 0))
    head = lambda w: pl.BlockSpec((1, H, tm, w), lambda b, i: (b, 0, i, 0))
    return pl.pallas_call(
        functools.partial(_mla_proj_kernel, q_rank=q_rank, kv_rank=kv_rank),
        grid=(B, S // tm),
        in_specs=[tok(D), full(g), full(w_in), full(qa_g), full(kva_g), full(w_uq), full(w_uk), full(w_uvt),
                  full(qn_g), full(qr_g), full(kn_g), full(kr_g), tok(LANES), tok(LANES)],
        out_specs=[head(qk_w), head(qk_w),
                   pl.BlockSpec((1, H, MLA_V_DIM, tm), lambda b, i: (b, 0, 0, i))],
        out_shape=[jax.ShapeDtypeStruct((B, H, S, qk_w), BF16),
                   jax.ShapeDtypeStruct((B, H, S, qk_w), BF16),
                   jax.ShapeDtypeStruct((B, H, MLA_V_DIM, S), BF16)],
        compiler_params=_params("parallel", "parallel"),
        name="mla_proj",
    )(x, g, w_in, qa_g, kva_g, w_uq, w_uk, w_uvt, qn_g, qr_g, kn_g, kr_g, cos_t, sin_t)


def _mla_attn_kernel(q_ref, k_ref, vt_ref, o_ref, s_scr, p_scr, *, tq, scale):
    S = q_ref.shape[2]
    key_i = lax.broadcasted_iota(jnp.int32, (tq, tq), 0)
    qry_i = lax.broadcasted_iota(jnp.int32, (tq, tq), 1)
    causal = key_i <= qry_i
    for i in range(S // tq):
        lo, hi, slot = i * tq, (i + 1) * tq, i % 2
        q = q_ref[0, 0, lo:hi, :]
        if i > 0:
            s_scr[slot, :lo, :] = _dot_nt(k_ref[0, 0, :lo, :], q) * (scale * LOG2E)
        s_d = _dot_nt(k_ref[0, 0, lo:hi, :], q) * (scale * LOG2E)
        s_scr[slot, lo:hi, :] = jnp.where(causal, s_d, NEG_INF)
        s = s_scr[slot, :hi, :]
        p = jnp.exp2(s - jnp.max(s, axis=0, keepdims=True))
        l = jnp.sum(p, axis=0, keepdims=True)
        p_scr[slot, :hi, :] = p.astype(BF16)
        acc = _dot(vt_ref[0, 0, :, :hi], p_scr[slot, :hi, :])
        o_ref[0, lo:hi, :] = (acc / l).T.astype(o_ref.dtype)


def _mla_attn(q, k, vt, *, tq):
    B, H, S, qk_w = q.shape
    dv = vt.shape[2]
    scale = 1.0 / math.sqrt(MLA_NOPE_DIM + MLA_ROPE_DIM)
    spec = lambda w: pl.BlockSpec((1, 1, S, w), lambda b, h: (b, h, 0, 0))
    return pl.pallas_call(
        functools.partial(_mla_attn_kernel, tq=tq, scale=scale),
        grid=(B, H),
        in_specs=[spec(qk_w), spec(qk_w), pl.BlockSpec((1, 1, dv, S), lambda b, h: (b, h, 0, 0))],
        out_specs=pl.BlockSpec((1, S, dv), lambda b, h: (b, 0, h)),
        out_shape=jax.ShapeDtypeStruct((B, S, H * dv), BF16),
        scratch_shapes=[pltpu.VMEM((2, S, tq), F32), pltpu.VMEM((2, S, tq), BF16)],
        compiler_params=_params("parallel", "parallel"),
        name="mla_attn",
    )(q, k, vt)


def _moba_proj_kernel(x_ref, g_ref, wqk_ref, wvt_ref, qg_ref, kg_ref, q_out, k_out, vt_out, km_out, *, blk):
    d_model = x_ref.shape[-1]
    n_heads = q_out.shape[1]
    tm = x_ref.shape[1]
    dh = q_out.shape[-1]
    hw = n_heads * dh
    for j in range(tm // blk):
        sl = slice(j * blk, (j + 1) * blk)
        hb = _rms(x_ref[0, sl, :], g_ref[...], d_model).astype(BF16)
        qk = _dot(hb, wqk_ref[...])
        vt = _dot_nt(wvt_ref[...], hb)
        for hd in range(n_heads):
            lo = hd * dh
            q_out[0, hd, sl, :] = _rms(qk[:, lo:lo + dh], qg_ref[...], dh).astype(BF16)
            k_h = _rms(qk[:, hw + lo:hw + lo + dh], kg_ref[...], dh)
            k_out[0, hd, sl, :] = k_h.astype(BF16)
            km_out[0, j, hd:hd + 1, :] = jnp.mean(k_h, axis=0, keepdims=True)
            vt_out[0, hd, j] = vt[lo:lo + dh, :].astype(BF16)


def _moba_proj(x, g, w_qk, w_vt, q_g, k_g, *, tm):
    B, S, D = x.shape
    H = MOBA_HEADS
    dh = q_g.shape[-1]
    blk = MOBA_BLOCK
    nb = S // blk

    def full(a):
        return pl.BlockSpec(a.shape, lambda b, i: (0,) * a.ndim)

    head = pl.BlockSpec((1, H, tm, dh), lambda b, i: (b, 0, i, 0))
    return pl.pallas_call(
        functools.partial(_moba_proj_kernel, blk=blk),
        grid=(B, S // tm),
        in_specs=[pl.BlockSpec((1, tm, D), lambda b, i: (b, i, 0)),
                  full(g), full(w_qk), full(w_vt), full(q_g), full(k_g)],
        out_specs=[head, head,
                   pl.BlockSpec((1, H, tm // blk, dh, blk), lambda b, i: (b, 0, i, 0, 0)),
                   pl.BlockSpec((1, tm // blk, H, dh), lambda b, i: (b, i, 0, 0))],
        out_shape=[jax.ShapeDtypeStruct((B, H, S, dh), BF16)] * 2
        + [jax.ShapeDtypeStruct((B, H, nb, dh, blk), BF16),
           jax.ShapeDtypeStruct((B, nb, H, dh), F32)],
        compiler_params=_params("parallel", "parallel"),
        name="moba_proj",
    )(x, g, w_qk, w_vt, q_g, k_g)


def _t5_bucket(dist, n_buckets):
    max_exact = n_buckets // 2
    dist = jnp.maximum(dist, 0)
    nf = jnp.maximum(dist, 1).astype(F32)
    large = max_exact + (jnp.log(nf / max_exact) / math.log(REL_MAX_DISTANCE / max_exact)
                         * (n_buckets - max_exact)).astype(jnp.int32)
    large = jnp.minimum(large, n_buckets - 1)
    return jnp.where(dist < max_exact, dist, large)


def _moba_attn_body(q_ref, k_ref, vt_ref, km_ref, o_ref, m_scr, l_scr, acc_scr, mask_scr, al_scr, t_scr,
                    s_scr, p_scr, *, prep, bias, top_k, scale):
    n_heads, blk, dh = q_ref.shape[1], q_ref.shape[2], q_ref.shape[3]
    nb = km_ref.shape[2]
    own = pl.program_id(1)
    c2 = scale * LOG2E

    def score_pass(n, causal=None):
        prep(n)
        start = pl.multiple_of(n * blk, blk)
        for hd in range(n_heads):
            s = _dot_nt(k_ref[0, hd, pl.ds(start, blk), :], q_ref[0, hd]) + bias(hd, n)
            if causal is not None:
                s = jnp.where(causal, s, NEG_INF)
            s_scr[hd] = s
            t_scr[hd] = jnp.max(s, axis=0, keepdims=True)

    key_i = lax.broadcasted_iota(jnp.int32, (blk, blk), 0)
    qry_i = lax.broadcasted_iota(jnp.int32, (blk, blk), 1)
    score_pass(own, key_i <= qry_i)
    for hd in range(n_heads):
        m = t_scr[hd]
        p = jnp.exp2((s_scr[hd] - m) * c2)
        m_scr[hd] = m
        l_scr[hd] = jnp.sum(p, axis=0, keepdims=True)
        p_scr[hd] = p.astype(BF16)
    for hd in range(n_heads):
        acc_scr[hd] = _dot(vt_ref[0, hd, own], p_scr[hd])

    n_iota = lax.broadcasted_iota(jnp.int32, (nb, blk), 0)
    for hd in range(n_heads):
        gate = _dot_nt(km_ref[0, hd], q_ref[0, hd].astype(F32), precision=lax.Precision.HIGHEST)
        cnt = jnp.zeros((nb, blk), F32)
        for mm in range(nb):
            g_m = gate[mm:mm + 1, :]
            above = (g_m > gate) | ((g_m == gate) & (mm < n_iota))
            cnt = cnt + jnp.where(above, jnp.where(mm < own, 1.0, 0.0), 0.0)
        keep = (cnt < top_k) & (n_iota < own)
        mask_scr[hd] = jnp.where(keep, 0.0, NEG_INF)

    def past_block(n, carry):
        score_pass(n)
        for hd in range(n_heads):
            mask = mask_scr[hd, pl.ds(n, 1), :]
            m_old = m_scr[hd]
            m_new = jnp.maximum(m_old, t_scr[hd] + mask)
            alpha = jnp.exp2((m_old - m_new) * c2)
            p = jnp.exp2((s_scr[hd] - (m_new - mask)) * c2)
            l_scr[hd] = alpha * l_scr[hd] + jnp.sum(p, axis=0, keepdims=True)
            p_scr[hd] = p.astype(BF16)
            m_scr[hd] = m_new
            al_scr[hd] = alpha
        for hd in range(n_heads):
            acc_scr[hd] = al_scr[hd] * acc_scr[hd] + _dot(vt_ref[0, hd, n], p_scr[hd])
        return carry

    lax.fori_loop(0, own, past_block, 0)
    for hd in range(n_heads):
        o_ref[0, :, hd * dh:(hd + 1) * dh] = (acc_scr[hd] / l_scr[hd]).T.astype(o_ref.dtype)


def _moba_attn_general_kernel(q_ref, k_ref, vt_ref, km_ref, pq_ref, pk_ref, tbl_ref, o_ref,
                              m_scr, l_scr, acc_scr, mask_scr, al_scr, t_scr, s_scr, p_scr, bkt_scr,
                              *, top_k, scale, n_buckets):
    blk = q_ref.shape[2]
    pq = pq_ref[0, 0]
    tbl_s = tbl_ref[...] * (1.0 / scale)

    def prep(n):
        pk = pk_ref[0, pl.ds(pl.multiple_of(n * blk, blk), blk), :]
        bkt_scr[...] = _t5_bucket(pq - pk, n_buckets)

    def bias(hd, n):
        tbl = jnp.broadcast_to(tbl_s[hd:hd + 1, :], (blk, LANES))
        return jnp.concatenate([_lane_gather(tbl, bkt_scr[:, i * LANES:(i + 1) * LANES])
                                for i in range(blk // LANES)], axis=1)

    _moba_attn_body(q_ref, k_ref, vt_ref, km_ref, o_ref, m_scr, l_scr, acc_scr, mask_scr, al_scr, t_scr,
                    s_scr, p_scr, prep=prep, bias=bias, top_k=top_k, scale=scale)


def _moba_attn_consecutive_kernel(q_ref, k_ref, vt_ref, km_ref, tt_ref, o_ref,
                                  m_scr, l_scr, acc_scr, mask_scr, al_scr, t_scr, s_scr, p_scr,
                                  *, top_k, scale):
    own = pl.program_id(1)

    def bias(hd, n):
        t0 = 2 * (own - n)
        mid = tt_ref[hd, t0 + 1]
        top = jnp.concatenate([mid, tt_ref[hd, t0 + 2]], axis=1)
        bot = jnp.concatenate([tt_ref[hd, t0], mid], axis=1)
        return jnp.concatenate([top, bot], axis=0)

    _moba_attn_body(q_ref, k_ref, vt_ref, km_ref, o_ref, m_scr, l_scr, acc_scr, mask_scr, al_scr, t_scr,
                    s_scr, p_scr, prep=lambda n: None, bias=bias, top_k=top_k, scale=scale)


def _bias_table_kernel(tbl_ref, tt_ref, *, scale, n_buckets):
    n_heads, n_t = tt_ref.shape[0], tt_ref.shape[1]
    j = lax.broadcasted_iota(jnp.int32, (LANES, LANES), 0)
    l = lax.broadcasted_iota(jnp.int32, (LANES, LANES), 1)
    tbl_s = tbl_ref[...] * (1.0 / scale)
    for t in range(n_t):
        bucket = _t5_bucket(LANES * (t - 1) + l - j, n_buckets)
        for hd in range(n_heads):
            tt_ref[hd, t] = _lane_gather(jnp.broadcast_to(tbl_s[hd:hd + 1, :], (LANES, LANES)), bucket)


def _moba_bias_table(tbl, n_buckets, nb, scale):
    return pl.pallas_call(
        functools.partial(_bias_table_kernel, scale=scale, n_buckets=n_buckets),
        out_shape=jax.ShapeDtypeStruct((tbl.shape[0], 2 * nb + 1, LANES, LANES), F32),
        name="moba_bias_table",
    )(tbl)


def _moba_attn(q, k, vt, k_mean, positions, tbl, tt, steps_by_one, *, n_buckets):
    B, H, S, dh = q.shape
    blk = MOBA_BLOCK
    nb = S // blk
    scale = 1.0 / math.sqrt(dh)
    top_k = min(MOBA_TOP_K, nb - 1)
    whole = lambda a: pl.BlockSpec((1,) + a.shape[1:], lambda b, c: (b,) + (0,) * (a.ndim - 1))
    const = lambda a: pl.BlockSpec(a.shape, lambda b, c: (0,) * a.ndim)
    common = dict(
        grid=(B, nb),
        out_specs=pl.BlockSpec((1, blk, H * dh), lambda b, c: (b, c, 0)),
        out_shape=jax.ShapeDtypeStruct((B, S, H * dh), BF16),
        compiler_params=_params("parallel", "arbitrary"))
    qkv_specs = [pl.BlockSpec((1, H, blk, dh), lambda b, c: (b, 0, c, 0)), whole(k), whole(vt), whole(k_mean)]
    scratch = [pltpu.VMEM((H, 1, blk), F32), pltpu.VMEM((H, 1, blk), F32),
               pltpu.VMEM((H, dh, blk), F32), pltpu.VMEM((H, nb, blk), F32),
               pltpu.VMEM((H, 1, blk), F32), pltpu.VMEM((H, 1, blk), F32),
               pltpu.VMEM((H, blk, blk), F32), pltpu.VMEM((H, blk, blk), BF16)]

    def general():
        pos_col = positions.reshape(B, S, 1)
        pos_blk = positions.reshape(B, nb, 1, blk)
        return pl.pallas_call(
            functools.partial(_moba_attn_general_kernel, top_k=top_k, scale=scale, n_buckets=n_buckets),
            in_specs=qkv_specs + [pl.BlockSpec((1, 1, 1, blk), lambda b, c: (b, c, 0, 0)),
                                  whole(pos_col), const(tbl)],
            scratch_shapes=scratch + [pltpu.VMEM((blk, blk), jnp.int32)],
            name="moba_attn_general", **common,
        )(q, k, vt, k_mean, pos_blk, pos_col, tbl)

    def consecutive():
        return pl.pallas_call(
            functools.partial(_moba_attn_consecutive_kernel, top_k=top_k, scale=scale),
            in_specs=qkv_specs + [const(tt)],
            scratch_shapes=scratch,
            name="moba_attn", **common,
        )(q, k, vt, k_mean, tt)

    return lax.cond(steps_by_one, consecutive, general)


def _oproj_mlp_kernel(x_ref, a_ref, wo_ref, g_ref, win_ref, wout_ref, o_ref, h_scr):
    @pl.when(pl.program_id(1) == 0)
    def _():
        rows = x_ref.shape[0] // PROLOGUE_CHUNKS
        for r in range(PROLOGUE_CHUNKS):
            sl = slice(r * rows, (r + 1) * rows)
            x1 = x_ref[sl, :] + _dot(a_ref[sl, :], wo_ref[...])
            o_ref[sl, :] = x1
            h_scr[sl, :] = _rms(x1, g_ref[...], x1.shape[-1]).astype(BF16)

    a = jnp.maximum(_dot(h_scr[...], win_ref[...]), 0.0)
    o_ref[...] += _dot((a * a).astype(BF16), wout_ref[...])


def _oproj_mlp(x, attn, w_o, g, w_in, w_out, *, tm, tf):
    T, D = x.shape
    F = w_in.shape[1]
    return pl.pallas_call(
        _oproj_mlp_kernel,
        grid=(T // tm, F // tf),
        in_specs=[pl.BlockSpec((tm, D), lambda i, j: (i, 0)),
                  pl.BlockSpec((tm, attn.shape[1]), lambda i, j: (i, 0)),
                  pl.BlockSpec(w_o.shape, lambda i, j: (0, 0)),
                  pl.BlockSpec(g.shape, lambda i, j: (0, 0)),
                  pl.BlockSpec((D, tf), lambda i, j: (0, j)),
                  pl.BlockSpec((tf, D), lambda i, j: (j, 0))],
        out_specs=pl.BlockSpec((tm, D), lambda i, j: (i, 0)),
        out_shape=jax.ShapeDtypeStruct((T, D), F32),
        scratch_shapes=[pltpu.VMEM((tm, D), BF16)],
        compiler_params=_params("parallel", "arbitrary"),
        name="oproj_mlp",
    )(x, attn, w_o, g, w_in, w_out)


def _gather_cols(w, src):
    src = np.asarray(src)
    cols = jnp.take(w, jnp.asarray(np.maximum(src, 0)), axis=-1)
    return jnp.where(jnp.asarray(src >= 0), cols, jnp.zeros((), w.dtype))


def _rope_group_src(base):
    half = MLA_ROPE_DIM // 2
    src = np.full((LANES,), -1, np.int64)
    src[:half] = base + np.arange(half)
    src[LANES // 2:LANES // 2 + half] = base + half + np.arange(half)
    return src


def _mla_layouts(q_rank, kv_rank):
    H = MLA_HEADS
    qk = MLA_NOPE_DIM + MLA_ROPE_DIM
    kvw = MLA_NOPE_DIM + MLA_V_DIM
    in_src = np.concatenate([np.arange(q_rank + kv_rank), _rope_group_src(q_rank + kv_rank)])
    uq_src = np.concatenate(
        [h * qk + np.arange(MLA_NOPE_DIM) for h in range(H)]
        + [_rope_group_src(h * qk + MLA_NOPE_DIM) for h in range(H)])
    uk_src = np.concatenate([h * kvw + np.arange(MLA_NOPE_DIM) for h in range(H)])
    uv_src = np.concatenate([h * kvw + MLA_NOPE_DIM + np.arange(MLA_V_DIM) for h in range(H)])
    return in_src, uq_src, uk_src, uv_src, _rope_group_src(0)


def kernel(x, positions, rel_bias_table, attn_norm, mlp_norm, mla_w_in, mla_q_a_norm, mla_kv_a_norm,
           mla_w_uq, mla_w_ukv, mla_q_nope_norm, mla_q_rope_norm, mla_k_nope_norm, mla_k_rope_norm,
           mla_w_o, moba_w_qkv, moba_q_norm, moba_k_norm, moba_w_o, mlp_w_in, mlp_w_out):
    B, S, D = x.shape
    depth = attn_norm.shape[0]
    q_rank = mla_q_a_norm.shape[-1]
    kv_rank = mla_kv_a_norm.shape[-1]
    assert S % MOBA_BLOCK == 0 and mla_w_in.shape[-1] == q_rank + kv_rank + MLA_ROPE_DIM
    in_src, uq_src, uk_src, uv_src, rope_src = _mla_layouts(q_rank, kv_rank)
    row = lambda v: v.reshape(1, -1)
    cos_t, sin_t = _rope_tables(positions)
    n_buckets = rel_bias_table.shape[0]
    assert MOBA_BLOCK == 2 * LANES and n_buckets <= LANES
    moba_scale = 1.0 / math.sqrt(moba_q_norm.shape[-1])
    tbl = jnp.pad(rel_bias_table.T, ((0, 0), (0, LANES - n_buckets)))
    tt = _moba_bias_table(tbl, n_buckets, S // MOBA_BLOCK, moba_scale)
    steps_by_one = jnp.all(positions[:, 1:] - positions[:, :-1] == 1)
    for i in range(depth):
        j = i // 2
        if i % 2 == 0:
            q, k, vt = _mla_proj(
                x, row(attn_norm[i]), _gather_cols(mla_w_in[j], in_src).astype(BF16),
                row(mla_q_a_norm[j]), row(mla_kv_a_norm[j]),
                _gather_cols(mla_w_uq[j], uq_src).astype(BF16),
                _gather_cols(mla_w_ukv[j], uk_src).astype(BF16),
                _gather_cols(mla_w_ukv[j], uv_src).T.astype(BF16),
                row(mla_q_nope_norm[j]), row(_gather_cols(mla_q_rope_norm[j], rope_src)),
                row(mla_k_nope_norm[j]), row(_gather_cols(mla_k_rope_norm[j], rope_src)),
                cos_t, sin_t, tm=512)
            attn = _mla_attn(q, k, vt, tq=256)
            w_o = mla_w_o[j]
        else:
            hw = MOBA_HEADS * moba_q_norm.shape[-1]
            q, k, vt, k_mean = _moba_proj(x, row(attn_norm[i]), moba_w_qkv[j][:, :2 * hw].astype(BF16),
                                          moba_w_qkv[j][:, 2 * hw:].T.astype(BF16),
                                          row(moba_q_norm[j]), row(moba_k_norm[j]), tm=512)
            attn = _moba_attn(q, k, vt, k_mean.transpose(0, 2, 1, 3), positions, tbl, tt, steps_by_one,
                              n_buckets=n_buckets)
            w_o = moba_w_o[j]
        x = _oproj_mlp(x.reshape(B * S, D), attn.reshape(B * S, -1), w_o.astype(BF16),
                       row(mlp_norm[i]), mlp_w_in[i].astype(BF16), mlp_w_out[i].astype(BF16),
                       tm=1024, tf=1024).reshape(B, S, D)
    return x
```
